```python
import math
import jax, jax.numpy as jnp
from jax import lax
import numpy as np

D_MODEL = 1024
BATCH = 4
SEQ = 8192
DEPTH = 1
DEC_BATCH = 128
DEC_SEQ = 1
PAST_LEN = 8192
PAGE_SIZE = 128

ATTN_WIDTH = D_MODEL // 2
CONV_WIDTH = D_MODEL - ATTN_WIDTH
N_HEADS = 4
HEAD_DIM = ATTN_WIDTH // (2 * N_HEADS)
V_DIM = 2 * HEAD_DIM
CONV_K = 3
D_FF = 4 * D_MODEL
ROPE_THETA = 10000.0
EPS = 1e-6
Q_BLOCK = 128
PROJ_WIDTH = 3 * ATTN_WIDTH + 3 * CONV_WIDTH
SPLITS = (ATTN_WIDTH, 2 * ATTN_WIDTH, 3 * ATTN_WIDTH,
          3 * ATTN_WIDTH + CONV_WIDTH, 3 * ATTN_WIDTH + 2 * CONV_WIDTH)

kernel_name = "hymba_diffattn_shortconv_decode_step"

F32 = jnp.float32
NEG = -1e30


def rmsnorm(x, g):
    xf = x.astype(F32)
    y = xf * lax.rsqrt(jnp.mean(xf * xf, axis=-1, keepdims=True) + EPS) * g.astype(F32)
    return y.astype(x.dtype)


def rope(x, pos):
    half = HEAD_DIM // 2
    inv = ROPE_THETA ** (-jnp.arange(half, dtype=F32) / half)
    ang = pos.astype(F32)[:, None] * inv[None, :]
    cos = jnp.cos(ang)[:, None, None, :]
    sin = jnp.sin(ang)[:, None, None, :]
    xf = x.astype(F32)
    x1, x2 = xf[..., :half], xf[..., half:]
    out = jnp.concatenate([x1 * cos - x2 * sin, x2 * cos + x1 * sin], axis=-1)
    return out.astype(x.dtype)


def mixer_inputs(x, g_norm, w_in, g_q, g_k, pos):
    B, T = x.shape[0], x.shape[1]
    n = rmsnorm(x, g_norm)
    z = jnp.einsum('btd,de->bte', n, w_in)
    q, k, v, gb, gc, h = jnp.split(z, SPLITS, axis=-1)
    q = rope(rmsnorm(q.reshape(B, T, N_HEADS, 2, HEAD_DIM), g_q), pos)
    k = rope(rmsnorm(k.reshape(B, T, N_HEADS, 2, HEAD_DIM), g_k), pos)
    v = v.reshape(B, T, N_HEADS, V_DIM)
    u = gc * h
    return q, k, v, gb, u


def diff_weights(s, lam):
    p = jax.nn.softmax(s, axis=-1)
    return p[:, :, 0] - lam * p[:, :, 1]


def prompt_attention(q, k, v, lam):
    B, S = q.shape[0], q.shape[1]
    nblk = S // Q_BLOCK
    scale = HEAD_DIM ** -0.5
    qb = q.reshape(B, nblk, Q_BLOCK, N_HEADS, 2, HEAD_DIM).swapaxes(0, 1)
    kpos = jnp.arange(S)

    def block(args):
        qi, i = args
        s = jnp.einsum('bqhcd,bkhcd->bhcqk', qi, k, preferred_element_type=F32) * scale
        qpos = i * Q_BLOCK + jnp.arange(Q_BLOCK)
        s = jnp.where(kpos[None, :] <= qpos[:, None], s, NEG)
        w = diff_weights(s, lam).astype(v.dtype)
        return jnp.einsum('bhqk,bkhe->bqhe', w, v)

    o = lax.map(block, (qb, jnp.arange(nblk)))
    return o.swapaxes(0, 1).reshape(B, S, N_HEADS, V_DIM)


def sample_attention(q, k_new, v_new, k_past, v_past, lam):
    T = q.shape[1]
    P = k_past.shape[1]
    scale = HEAD_DIM ** -0.5
    s_past = jnp.einsum('bqhcd,bkhcd->bhcqk', q, k_past, preferred_element_type=F32) * scale
    s_new = jnp.einsum('bqhcd,bkhcd->bhcqk', q, k_new, preferred_element_type=F32) * scale
    causal = jnp.arange(T)[None, :] <= jnp.arange(T)[:, None]
    s_new = jnp.where(causal, s_new, NEG)
    w = diff_weights(jnp.concatenate([s_past, s_new], axis=-1), lam).astype(v_new.dtype)
    return (jnp.einsum('bhqk,bkhe->bqhe', w[..., :P], v_past)
            + jnp.einsum('bhqk,bkhe->bqhe', w[..., P:], v_new))


def short_conv(u_ext, w_conv):
    T = u_ext.shape[1] - (CONV_K - 1)
    out = u_ext[:, 0:T] * w_conv[0]
    for j in range(1, CONV_K):
        out = out + u_ext[:, j:j + T] * w_conv[j]
    return out


def finish_layer(x, o, gb, c, lam_init, g_sub, w_out, g_mlp, w_up, w_down):
    B, T = x.shape[0], x.shape[1]
    a = (rmsnorm(o, g_sub) * (1.0 - lam_init)).reshape(B, T, ATTN_WIDTH)
    mix = jnp.concatenate([a, gb * c], axis=-1)
    h = x + jnp.einsum('bte,ed->btd', mix, w_out)
    z = jnp.einsum('btd,df->btf', rmsnorm(h, g_mlp), w_up)
    return h + jnp.einsum('btf,fd->btd', jnp.square(jax.nn.relu(z)), w_down)


def setup_inputs(seed: int = 0) -> dict:
    key = jax.random.key(seed)
    ks = jax.random.split(key, 24)
    n_pages = PAST_LEN // PAGE_SIZE
    n_pool = (DEC_BATCH * n_pages * 5) // 4
    nrm = jax.random.normal
    perm = jax.random.permutation(ks[0], n_pool)[:DEC_BATCH * n_pages]
    return {
        "x_prompt": nrm(ks[1], (BATCH, SEQ, D_MODEL), F32),
        "x_sample": nrm(ks[2], (DEC_BATCH, DEC_SEQ, D_MODEL), F32),
        "cache_k": nrm(ks[3], (DEPTH, n_pool, PAGE_SIZE, N_HEADS, 2, HEAD_DIM), F32),
        "cache_v": nrm(ks[4], (DEPTH, n_pool, PAGE_SIZE, N_HEADS, V_DIM), F32),
        "state_conv": nrm(ks[5], (DEPTH, DEC_BATCH, CONV_K - 1, CONV_WIDTH), F32),
        "page_table": perm.reshape(DEC_BATCH, n_pages).astype(jnp.int32),
        "g_attn_norm": 1.0 + 0.02 * nrm(ks[6], (DEPTH, D_MODEL), F32),
        "w_in": nrm(ks[7], (DEPTH, D_MODEL, PROJ_WIDTH), F32) * D_MODEL ** -0.5,
        "g_q": 1.0 + 0.02 * nrm(ks[8], (DEPTH, HEAD_DIM), F32),
        "g_k": 1.0 + 0.02 * nrm(ks[9], (DEPTH, HEAD_DIM), F32),
        "lam_q1": 0.1 * nrm(ks[10], (DEPTH, HEAD_DIM), F32),
        "lam_k1": 0.1 * nrm(ks[11], (DEPTH, HEAD_DIM), F32),
        "lam_q2": 0.1 * nrm(ks[12], (DEPTH, HEAD_DIM), F32),
        "lam_k2": 0.1 * nrm(ks[13], (DEPTH, HEAD_DIM), F32),
        "g_subln": 1.0 + 0.02 * nrm(ks[14], (DEPTH, V_DIM), F32),
        "w_conv": nrm(ks[15], (DEPTH, CONV_K, CONV_WIDTH), F32) * CONV_K ** -0.5,
        "w_out": nrm(ks[16], (DEPTH, D_MODEL, D_MODEL), F32) * D_MODEL ** -0.5,
        "g_mlp_norm": 1.0 + 0.02 * nrm(ks[17], (DEPTH, D_MODEL), F32),
        "w_up": nrm(ks[18], (DEPTH, D_MODEL, D_FF), F32) * D_MODEL ** -0.5,
        "w_down": nrm(ks[19], (DEPTH, D_FF, D_MODEL), F32) * D_FF ** -0.5,
    }


def reference(x_prompt, x_sample, cache_k, cache_v, state_conv, page_table,
              g_attn_norm, w_in, g_q, g_k, lam_q1, lam_k1, lam_q2, lam_k2, g_subln,
              w_conv, w_out, g_mlp_norm, w_up, w_down):
    pos_p = jnp.arange(x_prompt.shape[1])
    pos_s = PAST_LEN + jnp.arange(x_sample.shape[1])
    db = x_sample.shape[0]
    past = page_table.shape[1] * PAGE_SIZE
    xp, xs = x_prompt, x_sample
    kp, vp, cp, ksl, vsl, csl = [], [], [], [], [], []
    for l in range(DEPTH):
        lam_init = 0.8 - 0.6 * math.exp(-0.3 * l)
        lam = (jnp.exp(jnp.sum(lam_q1[l].astype(F32) * lam_k1[l].astype(F32)))
               - jnp.exp(jnp.sum(lam_q2[l].astype(F32) * lam_k2[l].astype(F32))) + lam_init)

        q, k, v, gb, u = mixer_inputs(xp, g_attn_norm[l], w_in[l], g_q[l], g_k[l], pos_p)
        o = prompt_attention(q, k, v, lam)
        u_ext = jnp.pad(u, ((0, 0), (CONV_K - 1, 0), (0, 0)))
        c = short_conv(u_ext, w_conv[l])
        xp = finish_layer(xp, o, gb, c, lam_init, g_subln[l], w_out[l],
                          g_mlp_norm[l], w_up[l], w_down[l])
        kp.append(k)
        vp.append(v)
        cp.append(u_ext[:, -(CONV_K - 1):])

        q, k, v, gb, u = mixer_inputs(xs, g_attn_norm[l], w_in[l], g_q[l], g_k[l], pos_s)
        k_past = cache_k[l][page_table].reshape(db, past, N_HEADS, 2, HEAD_DIM)
        v_past = cache_v[l][page_table].reshape(db, past, N_HEADS, V_DIM)
        o = sample_attention(q, k, v, k_past, v_past, lam)
        u_ext = jnp.concatenate([state_conv[l].astype(u.dtype), u], axis=1)
        c = short_conv(u_ext, w_conv[l])
        xs = finish_layer(xs, o, gb, c, lam_init, g_subln[l], w_out[l],
                          g_mlp_norm[l], w_up[l], w_down[l])
        ksl.append(k)
        vsl.append(v)
        csl.append(u_ext[:, -(CONV_K - 1):])

    return (xp, xs, jnp.stack(kp), jnp.stack(vp), jnp.stack(cp),
            jnp.stack(ksl), jnp.stack(vsl), jnp.stack(csl))
```

```python
import functools
import math

import jax
import jax.numpy as jnp
from jax import lax
from jax.experimental import pallas as pl
from jax.experimental.pallas import tpu as pltpu

F32 = jnp.float32
BF16 = jnp.bfloat16

PAGE_SIZE = 128
ROPE_THETA = 10000.0
EPS = 1e-6
NEG = -1e30

LANES = 128
SUBLANES = 8
VMEM_LIMIT_BYTES = 56 * 1024 * 1024


def _pick(n, target):
    t = min(n, target)
    while n % t:
        t -= 1
    return t


def _const_spec(shape, n_grid):
    zeros = (0,) * len(shape)
    if n_grid == 1:
        return pl.BlockSpec(shape, lambda a: zeros, pipeline_mode=pl.Buffered(1))
    if n_grid == 2:
        return pl.BlockSpec(shape, lambda a, b: zeros, pipeline_mode=pl.Buffered(1))
    return pl.BlockSpec(shape, lambda a, b, c: zeros, pipeline_mode=pl.Buffered(1))


def _lam_value(lq1, lk1, lq2, lk2, lam_init):
    s1 = jnp.sum(lq1 * lk1, axis=-1, keepdims=True)
    s2 = jnp.sum(lq2 * lk2, axis=-1, keepdims=True)
    return jnp.exp(s1) - jnp.exp(s2) + lam_init


def _group_rms_rope(z, gain, gmat, cos, sin):
    zz = z * z
    hi = zz.astype(BF16)
    lo = (zz - hi.astype(F32)).astype(BF16)
    g = gmat
    ms = (jnp.dot(hi, g, preferred_element_type=F32)
          + jnp.dot(lo, g, preferred_element_type=F32))
    zn = z * lax.rsqrt(ms + EPS) * gain
    t = z.shape[0]
    lane = lax.broadcasted_iota(jnp.int32, (t, LANES), 1)
    upper = (lane & 32) != 0
    outs = []
    for c in range(z.shape[1] // LANES):
        blk = zn[:, c * LANES:(c + 1) * LANES]
        swapped = jnp.where(upper, pltpu.roll(blk, 32, 1), pltpu.roll(blk, LANES - 32, 1))
        outs.append(blk * cos + swapped * sin)
    return outs


def _project(x, gat, win, gq, gk, gmat, cos, sin, aw):
    ms = jnp.mean(x * x, axis=-1, keepdims=True)
    n = (x * lax.rsqrt(ms + EPS) * gat).astype(BF16)
    z = jnp.dot(n, win, preferred_element_type=F32)
    q_blocks = _group_rms_rope(z[:, 0:aw], gq, gmat, cos, sin)
    k_blocks = _group_rms_rope(z[:, aw:2 * aw], gk, gmat, cos, sin)
    v = z[:, 2 * aw:3 * aw]
    cw = (z.shape[1] - 3 * aw) // 3
    gb = z[:, 3 * aw:3 * aw + cw]
    gc = z[:, 3 * aw + cw:3 * aw + 2 * cw]
    h = z[:, 3 * aw + 2 * cw:]
    return q_blocks, k_blocks, v, gb, gc * h


def _inproj_prompt_body(x_ref, cos_ref, sin_ref, gat_ref, gq_ref, gk_ref, win_ref, gmat_ref,
                        wconv_ref, q_ref, k_ref, v_ref, kb_ref, vb_ref, gbc_ref, convp_ref,
                        ubuf, *, tm, aw, scale):
    j = pl.program_id(1)
    q_blocks, k_blocks, v, gb, u = _project(
        x_ref[...], gat_ref[...], win_ref[...], gq_ref[...], gk_ref[...], gmat_ref[...],
        cos_ref[...], sin_ref[...], aw)
    for h, (qb, kb) in enumerate(zip(q_blocks, k_blocks)):
        q_ref[h] = (qb * scale).astype(BF16)
        k_ref[:, h * LANES:(h + 1) * LANES] = kb
        kb_ref[h] = kb.astype(BF16)
        vb_ref[h] = v[:, h * LANES:(h + 1) * LANES].astype(BF16)
    v_ref[...] = v

    @pl.when(j == 0)
    def _():
        ubuf[0:SUBLANES, :] = jnp.zeros((SUBLANES, ubuf.shape[1]), F32)

    ubuf[SUBLANES:SUBLANES + tm, :] = u
    w = wconv_ref[...]
    c = (ubuf[SUBLANES - 2:SUBLANES - 2 + tm, :] * w[0:1, :]
         + ubuf[SUBLANES - 1:SUBLANES - 1 + tm, :] * w[1:2, :]
         + u * w[2:3, :])
    gbc_ref[...] = (gb * c).astype(BF16)
    convp_ref[...] = ubuf[SUBLANES + tm - 2:SUBLANES + tm, :]
    ubuf[0:SUBLANES, :] = ubuf[tm:tm + SUBLANES, :]


def _inproj_prompt(x, cos, sin, gat, gq, gk, win, gmat, wconv, *, tm):
    b, s, d = x.shape
    aw = gmat.shape[0]
    cw = wconv.shape[1]
    nh = aw // LANES
    pw = win.shape[1]
    grid = (b, s // tm)
    body = functools.partial(_inproj_prompt_body, tm=tm, aw=aw, scale=(LANES // 2) ** -0.5)
    return pl.pallas_call(
        body,
        grid=grid,
        in_specs=[
            pl.BlockSpec((None, tm, d), lambda i, j: (i, j, 0)),
            pl.BlockSpec((tm, LANES), lambda i, j: (j, 0)),
            pl.BlockSpec((tm, LANES), lambda i, j: (j, 0)),
            _const_spec((1, d), 2),
            _const_spec((1, aw), 2),
            _const_spec((1, aw), 2),
            _const_spec((d, pw), 2),
            _const_spec((aw, aw), 2),
            _const_spec((3, cw), 2),
        ],
        out_specs=[
            pl.BlockSpec((None, nh, tm, LANES), lambda i, j: (i, 0, j, 0)),
            pl.BlockSpec((None, tm, aw), lambda i, j: (i, j, 0)),
            pl.BlockSpec((None, tm, aw), lambda i, j: (i, j, 0)),
            pl.BlockSpec((None, nh, tm, LANES), lambda i, j: (i, 0, j, 0)),
            pl.BlockSpec((None, nh, tm, LANES), lambda i, j: (i, 0, j, 0)),
            pl.BlockSpec((None, tm, cw), lambda i, j: (i, j, 0)),
            pl.BlockSpec((None, 2, cw), lambda i, j: (i, 0, 0)),
        ],
        out_shape=[
            jax.ShapeDtypeStruct((b, nh, s, LANES), BF16),
            jax.ShapeDtypeStruct((b, s, aw), F32),
            jax.ShapeDtypeStruct((b, s, aw), F32),
            jax.ShapeDtypeStruct((b, nh, s, LANES), BF16),
            jax.ShapeDtypeStruct((b, nh, s, LANES), BF16),
            jax.ShapeDtypeStruct((b, s, cw), BF16),
            jax.ShapeDtypeStruct((b, 2, cw), F32),
        ],
        scratch_shapes=[pltpu.VMEM((tm + SUBLANES, cw), F32)],
        compiler_params=pltpu.CompilerParams(
            dimension_semantics=("arbitrary", "arbitrary"),
            vmem_limit_bytes=VMEM_LIMIT_BYTES),
        name="inproj_prompt",
    )(x, cos, sin, gat, gq, gk, win, gmat, wconv)


def _inproj_decode_body(x_ref, cos_ref, sin_ref, gat_ref, gq_ref, gk_ref, win_ref, gmat_ref,
                        wconv_ref, st_ref, q_ref, k_ref, v_ref, gbc_ref, convs_ref, *, aw, scale):
    q_blocks, k_blocks, v, gb, u = _project(
        x_ref[...], gat_ref[...], win_ref[...], gq_ref[...], gk_ref[...], gmat_ref[...],
        cos_ref[...], sin_ref[...], aw)
    for h, (qb, kb) in enumerate(zip(q_blocks, k_blocks)):
        q_ref[:, h * LANES:(h + 1) * LANES] = (qb * scale).astype(BF16)
        k_ref[:, h * LANES:(h + 1) * LANES] = kb
    v_ref[...] = v
    cw = u.shape[1]
    w = wconv_ref[...]
    s0 = st_ref[:, 0:cw]
    s1 = st_ref[:, cw:2 * cw]
    c = s0 * w[0:1, :] + s1 * w[1:2, :] + u * w[2:3, :]
    gbc_ref[...] = (gb * c).astype(BF16)
    convs_ref[:, 0:cw] = s1
    convs_ref[:, cw:2 * cw] = u


def _inproj_decode(x, cos, sin, gat, gq, gk, win, gmat, wconv, state):
    n, d = x.shape
    aw = gmat.shape[0]
    cw = wconv.shape[1]
    nh = aw // LANES
    body = functools.partial(_inproj_decode_body, aw=aw, scale=(LANES // 2) ** -0.5)
    full = lambda shape: pl.BlockSpec(shape, lambda i: (0,) * len(shape))
    return pl.pallas_call(
        body,
        grid=(1,),
        in_specs=[full((n, d)), full((1, LANES)), full((1, LANES)), full((1, d)), full((1, aw)),
                  full((1, aw)), full(win.shape), full((aw, aw)), full((3, cw)), full((n, 2 * cw))],
        out_specs=[full((n, aw)), full((n, aw)), full((n, aw)), full((n, cw)), full((n, 2 * cw))],
        out_shape=[
            jax.ShapeDtypeStruct((n, aw), BF16),
            jax.ShapeDtypeStruct((n, aw), F32),
            jax.ShapeDtypeStruct((n, aw), F32),
            jax.ShapeDtypeStruct((n, cw), BF16),
            jax.ShapeDtypeStruct((n, 2 * cw), F32),
        ],
        compiler_params=pltpu.CompilerParams(
            dimension_semantics=("arbitrary",), vmem_limit_bytes=VMEM_LIMIT_BYTES),
        name="inproj_decode",
    )(x, cos, sin, gat, gq, gk, win, gmat, wconv, state)


def _subln(o, gsub, post_scale):
    ms = jnp.mean(o * o, axis=-1, keepdims=True)
    return o * lax.rsqrt(ms + EPS) * gsub * post_scale


def _attn_prompt_body(lq1_ref, lk1_ref, lq2_ref, lk2_ref, gsub_ref, q_ref, k_ref, v_ref, o_ref,
                      m_scr, l_scr, acc_scr, *, tq, tk, lam_init):
    i = pl.program_id(2)
    q = q_ref[...]
    lane = lax.broadcasted_iota(jnp.int32, q.shape, 1)
    half = LANES // 2
    qs = (jnp.where(lane < half, q, jnp.zeros_like(q)), jnp.where(lane >= half, q, jnp.zeros_like(q)))
    m_scr[...] = jnp.full(m_scr.shape, NEG, F32)
    l_scr[...] = jnp.zeros(l_scr.shape, F32)
    acc_scr[...] = jnp.zeros(acc_scr.shape, F32)
    reps = tk // LANES

    def step(j, masked):
        start = pl.multiple_of(j * tk, tk)
        kblk = k_ref[pl.ds(start, tk), :]
        vblk = v_ref[pl.ds(start, tk), :]
        for c in range(2):
            s = lax.dot_general(qs[c], kblk, (((1,), (1,)), ((), ())), preferred_element_type=F32)
            if masked:
                row = lax.broadcasted_iota(jnp.int32, s.shape, 0)
                col = lax.broadcasted_iota(jnp.int32, s.shape, 1)
                s = jnp.where(col <= row + masked[0], s, NEG)
            m_prev = m_scr[c]
            m_new = jnp.maximum(m_prev, jnp.max(s, axis=1, keepdims=True))
            alpha = jnp.exp(m_prev - m_new)
            p = jnp.exp(s - jnp.concatenate([m_new] * reps, axis=1))
            l_scr[c] = alpha * l_scr[c] + jnp.sum(p, axis=1, keepdims=True)
            acc_scr[c] = alpha * acc_scr[c] + jnp.dot(p.astype(BF16), vblk, preferred_element_type=F32)
            m_scr[c] = m_new

    per = tq // tk
    lax.fori_loop(0, i * per, lambda j, carry: (step(j, None), carry)[1], 0)
    for jj in range(per):
        step(i * per + jj, (-jj * tk,))

    lam = _lam_value(lq1_ref[...], lk1_ref[...], lq2_ref[...], lk2_ref[...], lam_init)
    o = acc_scr[0] / l_scr[0] - lam * (acc_scr[1] / l_scr[1])
    o_ref[...] = _subln(o, gsub_ref[...], 1.0 - lam_init).astype(BF16)


def _attn_prompt(lams, gsub, q, k, v, *, tq, tk, lam_init):
    b, nh, s, _ = q.shape
    hd = lams[0].shape[1]
    body = functools.partial(_attn_prompt_body, tq=tq, tk=tk, lam_init=lam_init)
    return pl.pallas_call(
        body,
        grid=(b, nh, s // tq),
        in_specs=[_const_spec((1, hd), 3)] * 4 + [
            _const_spec((1, LANES), 3),
            pl.BlockSpec((None, None, tq, LANES), lambda bi, h, i: (bi, h, i, 0)),
            pl.BlockSpec((None, None, s, LANES), lambda bi, h, i: (bi, h, 0, 0)),
            pl.BlockSpec((None, None, s, LANES), lambda bi, h, i: (bi, h, 0, 0)),
        ],
        out_specs=pl.BlockSpec((None, None, tq, LANES), lambda bi, h, i: (bi, h, i, 0)),
        out_shape=jax.ShapeDtypeStruct((b, nh, s, LANES), BF16),
        scratch_shapes=[pltpu.VMEM((2, tq, LANES), F32)] * 3,
        compiler_params=pltpu.CompilerParams(
            dimension_semantics=("arbitrary", "arbitrary", "arbitrary"),
            vmem_limit_bytes=VMEM_LIMIT_BYTES),
        name="attn_prompt",
    )(*lams, gsub, q, k, v)


def _attn_decode_body(pt_ref, lq1_ref, lk1_ref, lq2_ref, lk2_ref, gsub_ref, exp_ref, q_ref, kn_ref,
                      vn_ref, ck_hbm, cv_hbm, o_ref, kbuf, vbuf, ksem, vsem,
                      *, n_pages, cp, n_slots, lam_init):
    b = pl.program_id(0)
    nb = pl.num_programs(0)
    nc = n_pages // cp
    aw = q_ref.shape[-1]
    nh = aw // LANES

    def copies(sample, chunk, slot):
        out = []
        for p in range(cp):
            page = pt_ref[sample * n_pages + chunk * cp + p]
            out.append(pltpu.make_async_copy(ck_hbm.at[page], kbuf.at[slot, p], ksem.at[slot]))
            out.append(pltpu.make_async_copy(cv_hbm.at[page], vbuf.at[slot, p], vsem.at[slot]))
        return out

    def start(sample, chunk, slot):
        for cpy in copies(sample, chunk, slot):
            cpy.start()

    def wait(sample, chunk, slot):
        for cpy in copies(sample, chunk, slot):
            cpy.wait()

    @pl.when(b == 0)
    def _():
        start(0, 0, 0)

    q = q_ref[...].astype(F32)
    row = lax.broadcasted_iota(jnp.int32, (SUBLANES, aw), 0)
    lane = lax.broadcasted_iota(jnp.int32, (SUBLANES, aw), 1)
    group = lane // (LANES // 2)
    qbd = jnp.where(group == row, q, 0.0)
    qbd_b = qbd.astype(BF16)

    kn = kn_ref[...]
    vn = vn_ref[...]
    m = jnp.sum(qbd * kn, axis=1, keepdims=True)
    l = jnp.ones_like(m)
    row_v = lax.broadcasted_iota(jnp.int32, (SUBLANES, LANES), 0)
    acc = jnp.zeros((SUBLANES, LANES), F32)
    for h in range(nh):
        acc = jnp.where(row_v // 2 == h, vn[:, h * LANES:(h + 1) * LANES], acc)

    row_e = lax.broadcasted_iota(jnp.int32, (SUBLANES, nh * PAGE_SIZE), 0)
    lane_e = lax.broadcasted_iota(jnp.int32, (SUBLANES, nh * PAGE_SIZE), 1)
    own_head = lane_e % nh == row_e // 2
    expand = exp_ref[...]

    for c in range(nc):
        slot = c % n_slots
        nxt = (c + 1) % n_slots
        if c + 1 < nc:
            start(b, c + 1, nxt)
        else:
            @pl.when(b + 1 < nb)
            def _():
                start(b + 1, 0, nxt)
        wait(b, c, slot)
        s = [jnp.dot(qbd_b, kbuf[slot, p].astype(BF16), preferred_element_type=F32)
             for p in range(cp)]
        s_max = functools.reduce(jnp.maximum, s)
        m_new = jnp.maximum(m, jnp.max(s_max, axis=1, keepdims=True))
        alpha = jnp.exp(m - m_new)
        p_all = jnp.concatenate([jnp.exp(sp - m_new) for sp in s], axis=0)
        l = alpha * l + jnp.sum(
            jnp.sum(p_all.reshape(cp, SUBLANES, LANES), axis=0), axis=1, keepdims=True)
        w_all = jnp.dot(p_all.astype(BF16), expand, preferred_element_type=F32)
        pv = jnp.zeros((SUBLANES, LANES), F32)
        for p in range(cp):
            w_p = jnp.where(own_head, w_all[p * SUBLANES:(p + 1) * SUBLANES, :], 0.0).astype(BF16)
            pv = pv + jnp.dot(w_p, vbuf[slot, p].astype(BF16), preferred_element_type=F32)
        acc = alpha * acc + pv
        m = m_new

    lam = _lam_value(lq1_ref[...], lk1_ref[...], lq2_ref[...], lk2_ref[...], lam_init)
    t = (acc / l) * jnp.where(row_v % 2 == 0, 1.0, -lam)
    o = t + pltpu.roll(t, SUBLANES - 1, 0)
    a = _subln(o, gsub_ref[...], 1.0 - lam_init)
    for h in range(nh):
        o_ref[h] = a[2 * h:2 * h + 1, :].astype(BF16)


def _attn_decode(pt_flat, lams, gsub, expand, q, kn, vn, ck, cv, *, n_pages, cp, lam_init):
    n, _, aw = q.shape
    nh = aw // LANES
    hd = lams[0].shape[1]
    n_slots = 2
    body = functools.partial(_attn_decode_body, n_pages=n_pages, cp=cp, n_slots=n_slots,
                             lam_init=lam_init)
    const = lambda shape: pl.BlockSpec(shape, lambda i, pt: (0,) * len(shape))
    row = pl.BlockSpec((None, 1, aw), lambda i, pt: (i, 0, 0))
    grid_spec = pltpu.PrefetchScalarGridSpec(
        num_scalar_prefetch=1,
        grid=(n,),
        in_specs=[const((1, hd))] * 4 + [const((1, LANES)), const(expand.shape), row, row, row,
                                         pl.BlockSpec(memory_space=pl.ANY),
                                         pl.BlockSpec(memory_space=pl.ANY)],
        out_specs=pl.BlockSpec((nh, None, 1, LANES), lambda i, pt: (0, i, 0, 0)),
        scratch_shapes=[
            pltpu.VMEM((n_slots, cp) + ck.shape[1:], F32),
            pltpu.VMEM((n_slots, cp) + cv.shape[1:], F32),
            pltpu.SemaphoreType.DMA((n_slots,)),
            pltpu.SemaphoreType.DMA((n_slots,)),
        ],
    )
    return pl.pallas_call(
        body,
        grid_spec=grid_spec,
        out_shape=jax.ShapeDtypeStruct((nh, n, 1, LANES), BF16),
        compiler_params=pltpu.CompilerParams(
            dimension_semantics=("arbitrary",), vmem_limit_bytes=VMEM_LIMIT_BYTES),
        name="attn_decode",
    )(pt_flat, *lams, gsub, expand, q, kn, vn, ck, cv)


def _finish_body(x_ref, a_ref, gbc_ref, wout_ref, gmlp_ref, wup_ref, wdown_ref, y_ref, *, fc):
    nh = a_ref.shape[0]
    mix = jnp.concatenate([a_ref[h] for h in range(nh)] + [gbc_ref[...]], axis=1)
    h = x_ref[...] + jnp.dot(mix, wout_ref[...], preferred_element_type=F32)
    ms = jnp.mean(h * h, axis=-1, keepdims=True)
    hn = (h * lax.rsqrt(ms + EPS) * gmlp_ref[...]).astype(BF16)
    y = h
    dff = wup_ref.shape[1]
    for c in range(dff // fc):
        z = jnp.dot(hn, wup_ref[:, c * fc:(c + 1) * fc], preferred_element_type=F32)
        r = jnp.maximum(z, 0.0)
        y = y + jnp.dot((r * r).astype(BF16), wdown_ref[c * fc:(c + 1) * fc, :],
                        preferred_element_type=F32)
    y_ref[...] = y


def _finish(x, a, gbc, wout, gmlp, wup, wdown, *, tm):
    b, s, d = x.shape
    nh = a.shape[1]
    cw = gbc.shape[2]
    dff = wup.shape[1]
    body = functools.partial(_finish_body, fc=_pick(dff, 1024))
    return pl.pallas_call(
        body,
        grid=(b, s // tm),
        in_specs=[
            pl.BlockSpec((None, tm, d), lambda i, j: (i, j, 0)),
            pl.BlockSpec((None, nh, tm, LANES), lambda i, j: (i, 0, j, 0)),
            pl.BlockSpec((None, tm, cw), lambda i, j: (i, j, 0)),
            _const_spec((d, d), 2),
            _const_spec((1, d), 2),
            _const_spec((d, dff), 2),
            _const_spec((dff, d), 2),
        ],
        out_specs=pl.BlockSpec((None, tm, d), lambda i, j: (i, j, 0)),
        out_shape=jax.ShapeDtypeStruct((b, s, d), F32),
        compiler_params=pltpu.CompilerParams(
            dimension_semantics=("arbitrary", "arbitrary"),
            vmem_limit_bytes=VMEM_LIMIT_BYTES),
        name="finish",
    )(x, a, gbc, wout, gmlp, wup, wdown)


def _rope_tables(pos, half):
    inv = ROPE_THETA ** (-jnp.arange(half, dtype=F32) / half)
    ang = pos.astype(F32)[:, None] * inv[None, :]
    cos = jnp.cos(ang)
    sin = jnp.sin(ang)
    return (jnp.concatenate([cos, cos, cos, cos], axis=1),
            jnp.concatenate([-sin, sin, -sin, sin], axis=1))


def kernel(x_prompt, x_sample, cache_k, cache_v, state_conv, page_table, g_attn_norm, w_in, g_q,
           g_k, lam_q1, lam_k1, lam_q2, lam_k2, g_subln, w_conv, w_out, g_mlp_norm, w_up, w_down):
    b, s, d = x_prompt.shape
    db, dec_seq, _ = x_sample.shape
    assert dec_seq == 1, "decode path handles one new token per sample"
    depth = w_in.shape[0]
    hd = g_q.shape[-1]
    vd = g_subln.shape[-1]
    assert vd == LANES and 2 * hd == LANES, "one head must fill one 128-lane block"
    aw = d // 2
    cw = d - aw
    nh = aw // vd
    n_pool = cache_k.shape[1]
    n_pages = page_table.shape[1]
    past = n_pages * PAGE_SIZE
    assert cache_k.shape[2] == PAGE_SIZE

    tm = _pick(s, 512)
    tq = _pick(s, 512)
    assert n_pages % 2 == 0
    cp = _pick(n_pages // 2, 16)

    cos_p, sin_p = _rope_tables(jnp.arange(s), hd // 2)
    cos_s, sin_s = _rope_tables(past + jnp.arange(1), hd // 2)
    grp = jnp.arange(aw) // hd
    gmat = ((grp[:, None] == grp[None, :]).astype(F32) / hd).astype(BF16)

    ck = jnp.transpose(cache_k, (0, 1, 3, 4, 5, 2)).reshape(depth * n_pool, aw, PAGE_SIZE)
    cv = cache_v.reshape(depth * n_pool, PAGE_SIZE * nh, vd)
    slot_of = jnp.arange(PAGE_SIZE * nh) // nh
    expand = (jnp.arange(PAGE_SIZE)[:, None] == slot_of[None, :]).astype(BF16)

    xp = x_prompt
    xs = x_sample.reshape(db, d)
    kp, vp, cpr, ksl, vsl, csl = [], [], [], [], [], []
    for l in range(depth):
        lam_init = 0.8 - 0.6 * math.exp(-0.3 * l)
        lams = tuple(v[l].reshape(1, hd).astype(F32) for v in (lam_q1, lam_k1, lam_q2, lam_k2))
        gat = g_attn_norm[l].reshape(1, d)
        gq = jnp.tile(g_q[l], aw // hd).reshape(1, aw)
        gk = jnp.tile(g_k[l], aw // hd).reshape(1, aw)
        gsub = g_subln[l].reshape(1, vd)
        gmlp = g_mlp_norm[l].reshape(1, d)
        win = w_in[l].astype(BF16)
        wout = w_out[l].astype(BF16)
        wup = w_up[l].astype(BF16)
        wdown = w_down[l].astype(BF16)
        wconv = w_conv[l]

        q, k, v, kb, vb, gbc, convp = _inproj_prompt(xp, cos_p, sin_p, gat, gq, gk, win, gmat,
                                                     wconv, tm=tm)
        a = _attn_prompt(lams, gsub, q, kb, vb, tq=tq, tk=tq, lam_init=lam_init)
        xp = _finish(xp, a, gbc, wout, gmlp, wup, wdown, tm=tm)
        kp.append(k.reshape(b, s, nh, 2, hd))
        vp.append(v.reshape(b, s, nh, vd))
        cpr.append(convp)

        qd, kn, vn, gbcd, convs = _inproj_decode(xs, cos_s, sin_s, gat, gq, gk, win, gmat, wconv,
                                                 state_conv[l].reshape(db, 2 * cw))
        pt_flat = (page_table + l * n_pool).reshape(-1).astype(jnp.int32)
        ad = _attn_decode(pt_flat, lams, gsub, expand, qd.reshape(db, 1, aw), kn.reshape(db, 1, aw),
                          vn.reshape(db, 1, aw), ck, cv, n_pages=n_pages, cp=cp,
                          lam_init=lam_init)
        xs = _finish(xs.reshape(1, db, d), ad.reshape(1, nh, db, LANES), gbcd.reshape(1, db, cw),
                     wout, gmlp, wup, wdown, tm=db).reshape(db, d)
        ksl.append(kn.reshape(db, 1, nh, 2, hd))
        vsl.append(vn.reshape(db, 1, nh, vd))
        csl.append(convs.reshape(db, 2, cw))

    return (xp, xs.reshape(db, 1, d), jnp.stack(kp), jnp.stack(vp), jnp.stack(cpr),
            jnp.stack(ksl), jnp.stack(vsl), jnp.stack(csl))
```

```python
import functools
import math

import jax
import jax.numpy as jnp
from jax import lax
from jax.experimental import pallas as pl
from jax.experimental.pallas import tpu as pltpu

F32 = jnp.float32
BF16 = jnp.bfloat16

PAGE_SIZE = 128
ROPE_THETA = 10000.0
EPS = 1e-6
NEG = -1e30

LANES = 128
SUBLANES = 8
VMEM_LIMIT_BYTES = 56 * 1024 * 1024


def _pick(n, target):
    t = min(n, target)
    while n % t:
        t -= 1
    return t


def _const_spec(shape, n_grid):
    zeros = (0,) * len(shape)
    if n_grid == 1:
        return pl.BlockSpec(shape, lambda a: zeros, pipeline_mode=pl.Buffered(1))
    if n_grid == 2:
        return pl.BlockSpec(shape, lambda a, b: zeros, pipeline_mode=pl.Buffered(1))
    return pl.BlockSpec(shape, lambda a, b, c: zeros, pipeline_mode=pl.Buffered(1))


def _lam_value(lq1, lk1, lq2, lk2, lam_init):
    s1 = jnp.sum(lq1 * lk1, axis=-1, keepdims=True)
    s2 = jnp.sum(lq2 * lk2, axis=-1, keepdims=True)
    return jnp.exp(s1) - jnp.exp(s2) + lam_init


def _group_rms_rope(z, gain, gmat, cos, sin):
    zz = z * z
    hi = zz.astype(BF16)
    lo = (zz - hi.astype(F32)).astype(BF16)
    g = gmat
    ms = (jnp.dot(hi, g, preferred_element_type=F32)
          + jnp.dot(lo, g, preferred_element_type=F32))
    zn = z * lax.rsqrt(ms + EPS) * gain
    t = z.shape[0]
    lane = lax.broadcasted_iota(jnp.int32, (t, LANES), 1)
    upper = (lane & 32) != 0
    outs = []
    for c in range(z.shape[1] // LANES):
        blk = zn[:, c * LANES:(c + 1) * LANES]
        swapped = jnp.where(upper, pltpu.roll(blk, 32, 1), pltpu.roll(blk, LANES - 32, 1))
        outs.append(blk * cos + swapped * sin)
    return outs


def _project(x, gat, win, gq, gk, gmat, cos, sin, aw):
    ms = jnp.mean(x * x, axis=-1, keepdims=True)
    n = (x * lax.rsqrt(ms + EPS) * gat).astype(BF16)
    z = jnp.dot(n, win, preferred_element_type=F32)
    q_blocks = _group_rms_rope(z[:, 0:aw], gq, gmat, cos, sin)
    k_blocks = _group_rms_rope(z[:, aw:2 * aw], gk, gmat, cos, sin)
    v = z[:, 2 * aw:3 * aw]
    cw = (z.shape[1] - 3 * aw) // 3
    gb = z[:, 3 * aw:3 * aw + cw]
    gc = z[:, 3 * aw + cw:3 * aw + 2 * cw]
    h = z[:, 3 * aw + 2 * cw:]
    return q_blocks, k_blocks, v, gb, gc * h


def _inproj_prompt_body(x_ref, cos_ref, sin_ref, gat_ref, gq_ref, gk_ref, win_ref, gmat_ref,
                        wconv_ref, qt_ref, k_ref, v_ref, kb_ref, vt_ref, gbc_ref, convp_ref,
                        ubuf, *, tm, aw, scale):
    j = pl.program_id(1)
    nh = aw // LANES
    q_blocks, k_blocks, v, gb, u = _project(
        x_ref[...], gat_ref[...], win_ref[...], gq_ref[...], gk_ref[...], gmat_ref[...],
        cos_ref[...], sin_ref[...], aw)
    for h, (qb, kb) in enumerate(zip(q_blocks, k_blocks)):
        vh = v[:, h * LANES:(h + 1) * LANES]
        qt_ref[h] = (qb * scale).T.astype(BF16)
        vt_ref[h] = vh.T.astype(BF16)
        k_ref[:, h * LANES:(h + 1) * LANES] = kb
        kb_ref[h] = kb.astype(BF16)
        v_ref[pl.ds(h, tm, stride=nh), :] = vh

    @pl.when(j == 0)
    def _():
        ubuf[0:SUBLANES, :] = jnp.zeros((SUBLANES, ubuf.shape[1]), F32)

    ubuf[SUBLANES:SUBLANES + tm, :] = u
    w = wconv_ref[...]
    c = (ubuf[SUBLANES - 2:SUBLANES - 2 + tm, :] * w[0:1, :]
         + ubuf[SUBLANES - 1:SUBLANES - 1 + tm, :] * w[1:2, :]
         + u * w[2:3, :])
    gbc_ref[...] = (gb * c).astype(BF16)
    convp_ref[...] = ubuf[SUBLANES + tm - 2:SUBLANES + tm, :]
    ubuf[0:SUBLANES, :] = ubuf[tm:tm + SUBLANES, :]


def _inproj_prompt(x, cos, sin, gat, gq, gk, win, gmat, wconv, *, tm):
    b, s, d = x.shape
    aw = gmat.shape[0]
    cw = wconv.shape[1]
    nh = aw // LANES
    pw = win.shape[1]
    grid = (b, s // tm)
    body = functools.partial(_inproj_prompt_body, tm=tm, aw=aw,
                             scale=(LANES // 2) ** -0.5 * math.log2(math.e))
    return pl.pallas_call(
        body,
        grid=grid,
        in_specs=[
            pl.BlockSpec((None, tm, d), lambda i, j: (i, j, 0)),
            pl.BlockSpec((tm, LANES), lambda i, j: (j, 0)),
            pl.BlockSpec((tm, LANES), lambda i, j: (j, 0)),
            _const_spec((1, d), 2),
            _const_spec((1, aw), 2),
            _const_spec((1, aw), 2),
            _const_spec((d, pw), 2),
            _const_spec((aw, aw), 2),
            _const_spec((3, cw), 2),
        ],
        out_specs=[
            pl.BlockSpec((None, nh, LANES, tm), lambda i, j: (i, 0, 0, j)),
            pl.BlockSpec((None, tm, aw), lambda i, j: (i, j, 0)),
            pl.BlockSpec((None, tm * nh, LANES), lambda i, j: (i, j, 0)),
            pl.BlockSpec((None, nh, tm, LANES), lambda i, j: (i, 0, j, 0)),
            pl.BlockSpec((None, nh, LANES, tm), lambda i, j: (i, 0, 0, j)),
            pl.BlockSpec((None, tm, cw), lambda i, j: (i, j, 0)),
            pl.BlockSpec((None, 2, cw), lambda i, j: (i, 0, 0)),
        ],
        out_shape=[
            jax.ShapeDtypeStruct((b, nh, LANES, s), BF16),
            jax.ShapeDtypeStruct((b, s, aw), F32),
            jax.ShapeDtypeStruct((b, s * nh, LANES), F32),
            jax.ShapeDtypeStruct((b, nh, s, LANES), BF16),
            jax.ShapeDtypeStruct((b, nh, LANES, s), BF16),
            jax.ShapeDtypeStruct((b, s, cw), BF16),
            jax.ShapeDtypeStruct((b, 2, cw), F32),
        ],
        scratch_shapes=[pltpu.VMEM((tm + SUBLANES, cw), F32)],
        compiler_params=pltpu.CompilerParams(
            dimension_semantics=("arbitrary", "arbitrary"),
            vmem_limit_bytes=VMEM_LIMIT_BYTES),
        name="inproj_prompt",
    )(x, cos, sin, gat, gq, gk, win, gmat, wconv)


def _inproj_decode_body(x_ref, cos_ref, sin_ref, gat_ref, gq_ref, gk_ref, win_ref, gmat_ref,
                        wconv_ref, st_ref, q_ref, k_ref, v_ref, gbc_ref, convs_ref, *, aw, scale):
    q_blocks, k_blocks, v, gb, u = _project(
        x_ref[...], gat_ref[...], win_ref[...], gq_ref[...], gk_ref[...], gmat_ref[...],
        cos_ref[...], sin_ref[...], aw)
    for h, (qb, kb) in enumerate(zip(q_blocks, k_blocks)):
        q_ref[:, h * LANES:(h + 1) * LANES] = (qb * scale).astype(BF16)
        k_ref[:, h * LANES:(h + 1) * LANES] = kb
    v_ref[...] = v
    cw = u.shape[1]
    w = wconv_ref[...]
    s0 = st_ref[:, 0:cw]
    s1 = st_ref[:, cw:2 * cw]
    c = s0 * w[0:1, :] + s1 * w[1:2, :] + u * w[2:3, :]
    gbc_ref[...] = (gb * c).astype(BF16)
    convs_ref[:, 0:cw] = s1
    convs_ref[:, cw:2 * cw] = u


def _inproj_decode(x, cos, sin, gat, gq, gk, win, gmat, wconv, state):
    n, d = x.shape
    aw = gmat.shape[0]
    cw = wconv.shape[1]
    nh = aw // LANES
    body = functools.partial(_inproj_decode_body, aw=aw, scale=(LANES // 2) ** -0.5)
    full = lambda shape: pl.BlockSpec(shape, lambda i: (0,) * len(shape))
    return pl.pallas_call(
        body,
        grid=(1,),
        in_specs=[full((n, d)), full((1, LANES)), full((1, LANES)), full((1, d)), full((1, aw)),
                  full((1, aw)), full(win.shape), full((aw, aw)), full((3, cw)), full((n, 2 * cw))],
        out_specs=[full((n, aw)), full((n, aw)), full((n, aw)), full((n, cw)), full((n, 2 * cw))],
        out_shape=[
            jax.ShapeDtypeStruct((n, aw), BF16),
            jax.ShapeDtypeStruct((n, aw), F32),
            jax.ShapeDtypeStruct((n, aw), F32),
            jax.ShapeDtypeStruct((n, cw), BF16),
            jax.ShapeDtypeStruct((n, 2 * cw), F32),
        ],
        compiler_params=pltpu.CompilerParams(
            dimension_semantics=("arbitrary",), vmem_limit_bytes=VMEM_LIMIT_BYTES),
        name="inproj_decode",
    )(x, cos, sin, gat, gq, gk, win, gmat, wconv, state)


def _subln(o, gsub, post_scale):
    ms = jnp.mean(o * o, axis=-1, keepdims=True)
    return o * lax.rsqrt(ms + EPS) * gsub * post_scale


def _attn_prompt_body(lq1_ref, lk1_ref, lq2_ref, lk2_ref, gsub_ref, qt_ref, k_ref, vt_ref, o_ref,
                      qz_scr, s_scr, mb_scr, m_scr, l_scr, acc_scr, *, tq, tk, lam_init):
    assert tq == 2 * tk
    i = pl.program_id(2)
    n2 = 2 * tq
    qt = qt_ref[...]
    sub = lax.broadcasted_iota(jnp.int32, qt.shape, 0)
    zero = jnp.zeros_like(qt)
    qz_scr[:, 0:tq] = jnp.where(sub < LANES // 2, qt, zero)
    qz_scr[:, tq:n2] = jnp.where(sub >= LANES // 2, qt, zero)
    m_scr[...] = jnp.full(m_scr.shape, NEG, F32)
    l_scr[...] = jnp.zeros(l_scr.shape, F32)
    acc_scr[...] = jnp.zeros(acc_scr.shape, F32)

    def stage_a(start, buf, diag_offset):
        kblk = k_ref[pl.ds(pl.multiple_of(start, tk), tk), :]
        s = jnp.dot(kblk, qz_scr[...], preferred_element_type=F32)
        if diag_offset is not None:
            key = lax.broadcasted_iota(jnp.int32, s.shape, 0) + diag_offset
            col = lax.broadcasted_iota(jnp.int32, s.shape, 1)
            qry = jnp.where(col >= tq, col - tq, col)
            s = jnp.where(key <= qry, s, NEG)
        s_scr[buf] = s
        mb_scr[buf] = jnp.max(s, axis=0, keepdims=True)

    def stage_b(start, buf):
        m_prev = m_scr[...]
        m_new = jnp.maximum(m_prev, mb_scr[buf])
        alpha = jnp.exp2(m_prev - m_new)
        p = jnp.exp2(s_scr[buf] - m_new)
        l_scr[...] = alpha * l_scr[...] + jnp.sum(p, axis=0, keepdims=True)
        vt = vt_ref[:, pl.ds(pl.multiple_of(start, tk), tk)]
        acc_scr[...] = alpha * acc_scr[...] + jnp.dot(vt, p.astype(BF16),
                                                      preferred_element_type=F32)
        m_scr[...] = m_new

    diag0 = i * tq
    stage_a(diag0, 0, 0)

    def pair(u, carry):
        first = 2 * u * tk
        stage_a(first, 1, None)
        stage_b(jnp.where(u == 0, diag0, first - tk), 0)
        stage_a(first + tk, 0, None)
        stage_b(first, 1)
        return carry

    lax.fori_loop(0, i, pair, 0)
    stage_a(diag0 + tk, 1, tk)
    stage_b(jnp.where(i == 0, diag0, diag0 - tk), 0)
    stage_b(diag0 + tk, 1)

    lam = _lam_value(lq1_ref[...], lk1_ref[...], lq2_ref[...], lk2_ref[...], lam_init)
    ot = (acc_scr[:, 0:tq] / l_scr[:, 0:tq]
          - lam * (acc_scr[:, tq:n2] / l_scr[:, tq:n2]))
    ms = jnp.mean(ot * ot, axis=0, keepdims=True)
    at = ot * lax.rsqrt(ms + EPS) * gsub_ref[...] * (1.0 - lam_init)
    o_ref[...] = at.T.astype(BF16)


def _attn_prompt(lams, gsub_col, qt, k, vt, *, tq, tk, lam_init):
    b, nh, s, _ = k.shape
    hd = lams[0].shape[1]
    body = functools.partial(_attn_prompt_body, tq=tq, tk=tk, lam_init=lam_init)
    return pl.pallas_call(
        body,
        grid=(b, nh, s // tq),
        in_specs=[_const_spec((1, hd), 3)] * 4 + [
            _const_spec((LANES, 1), 3),
            pl.BlockSpec((None, None, LANES, tq), lambda bi, h, i: (bi, h, 0, i)),
            pl.BlockSpec((None, None, s, LANES), lambda bi, h, i: (bi, h, 0, 0)),
            pl.BlockSpec((None, None, LANES, s), lambda bi, h, i: (bi, h, 0, 0)),
        ],
        out_specs=pl.BlockSpec((None, None, tq, LANES), lambda bi, h, i: (bi, h, i, 0)),
        out_shape=jax.ShapeDtypeStruct((b, nh, s, LANES), BF16),
        scratch_shapes=[pltpu.VMEM((LANES, 2 * tq), BF16),
                        pltpu.VMEM((2, tk, 2 * tq), F32),
                        pltpu.VMEM((2, 1, 2 * tq), F32),
                        pltpu.VMEM((1, 2 * tq), F32),
                        pltpu.VMEM((1, 2 * tq), F32),
                        pltpu.VMEM((LANES, 2 * tq), F32)],
        compiler_params=pltpu.CompilerParams(
            dimension_semantics=("arbitrary", "arbitrary", "arbitrary"),
            vmem_limit_bytes=VMEM_LIMIT_BYTES),
        name="attn_prompt",
    )(*lams, gsub_col, qt, k, vt)


def _attn_decode_body(pt_ref, lq1_ref, lk1_ref, lq2_ref, lk2_ref, gsub_ref, exp_ref, q_ref, kn_ref,
                      vn_ref, ck_hbm, cv_hbm, o_ref, kbuf, vbuf, ksem, vsem,
                      *, n_pages, cp, n_slots, lam_init):
    b = pl.program_id(0)
    nb = pl.num_programs(0)
    nc = n_pages // cp
    aw = q_ref.shape[-1]
    nh = aw // LANES

    def copies(sample, chunk, slot):
        out = []
        for p in range(cp):
            page = pt_ref[sample * n_pages + chunk * cp + p]
            out.append(pltpu.make_async_copy(ck_hbm.at[page], kbuf.at[slot, p], ksem.at[slot]))
            out.append(pltpu.make_async_copy(cv_hbm.at[page], vbuf.at[slot, p], vsem.at[slot]))
        return out

    def start(sample, chunk, slot):
        for cpy in copies(sample, chunk, slot):
            cpy.start()

    def wait(sample, chunk, slot):
        for cpy in copies(sample, chunk, slot):
            cpy.wait()

    @pl.when(b == 0)
    def _():
        start(0, 0, 0)

    q = q_ref[...].astype(F32)
    row = lax.broadcasted_iota(jnp.int32, (SUBLANES, aw), 0)
    lane = lax.broadcasted_iota(jnp.int32, (SUBLANES, aw), 1)
    group = lane // (LANES // 2)
    qbd = jnp.where(group == row, q, 0.0)
    qbd_b = qbd.astype(BF16)

    kn = kn_ref[...]
    vn = vn_ref[...]
    m = jnp.sum(qbd * kn, axis=1, keepdims=True)
    l = jnp.ones_like(m)
    row_v = lax.broadcasted_iota(jnp.int32, (SUBLANES, LANES), 0)
    acc = jnp.zeros((SUBLANES, LANES), F32)
    for h in range(nh):
        acc = jnp.where(row_v // 2 == h, vn[:, h * LANES:(h + 1) * LANES], acc)

    row_e = lax.broadcasted_iota(jnp.int32, (SUBLANES, nh * PAGE_SIZE), 0)
    lane_e = lax.broadcasted_iota(jnp.int32, (SUBLANES, nh * PAGE_SIZE), 1)
    own_head = lane_e % nh == row_e // 2
    expand = exp_ref[...]

    for c in range(nc):
        slot = c % n_slots
        nxt = (c + 1) % n_slots
        if c + 1 < nc:
            start(b, c + 1, nxt)
        else:
            @pl.when(b + 1 < nb)
            def _():
                start(b + 1, 0, nxt)
        wait(b, c, slot)
        s = [jnp.dot(qbd_b, kbuf[slot, p].astype(BF16), preferred_element_type=F32)
             for p in range(cp)]
        s_max = functools.reduce(jnp.maximum, s)
        m_new = jnp.maximum(m, jnp.max(s_max, axis=1, keepdims=True))
        alpha = jnp.exp(m - m_new)
        p_all = jnp.concatenate([jnp.exp(sp - m_new) for sp in s], axis=0)
        l = alpha * l + jnp.sum(
            jnp.sum(p_all.reshape(cp, SUBLANES, LANES), axis=0), axis=1, keepdims=True)
        w_all = jnp.dot(p_all.astype(BF16), expand, preferred_element_type=F32)
        pv = jnp.zeros((SUBLANES, LANES), F32)
        for p in range(cp):
            w_p = jnp.where(own_head, w_all[p * SUBLANES:(p + 1) * SUBLANES, :], 0.0).astype(BF16)
            pv = pv + jnp.dot(w_p, vbuf[slot, p].astype(BF16), preferred_element_type=F32)
        acc = alpha * acc + pv
        m = m_new

    lam = _lam_value(lq1_ref[...], lk1_ref[...], lq2_ref[...], lk2_ref[...], lam_init)
    t = (acc / l) * jnp.where(row_v % 2 == 0, 1.0, -lam)
    o = t + pltpu.roll(t, SUBLANES - 1, 0)
    a = _subln(o, gsub_ref[...], 1.0 - lam_init)
    for h in range(nh):
        o_ref[h] = a[2 * h:2 * h + 1, :].astype(BF16)


def _attn_decode(pt_flat, lams, gsub, expand, q, kn, vn, ck, cv, *, n_pages, cp, lam_init):
    n, _, aw = q.shape
    nh = aw // LANES
    hd = lams[0].shape[1]
    n_slots = 2
    body = functools.partial(_attn_decode_body, n_pages=n_pages, cp=cp, n_slots=n_slots,
                             lam_init=lam_init)
    const = lambda shape: pl.BlockSpec(shape, lambda i, pt: (0,) * len(shape))
    row = pl.BlockSpec((None, 1, aw), lambda i, pt: (i, 0, 0))
    grid_spec = pltpu.PrefetchScalarGridSpec(
        num_scalar_prefetch=1,
        grid=(n,),
        in_specs=[const((1, hd))] * 4 + [const((1, LANES)), const(expand.shape), row, row, row,
                                         pl.BlockSpec(memory_space=pl.ANY),
                                         pl.BlockSpec(memory_space=pl.ANY)],
        out_specs=pl.BlockSpec((nh, None, 1, LANES), lambda i, pt: (0, i, 0, 0)),
        scratch_shapes=[
            pltpu.VMEM((n_slots, cp) + ck.shape[1:], F32),
            pltpu.VMEM((n_slots, cp) + cv.shape[1:], F32),
            pltpu.SemaphoreType.DMA((n_slots,)),
            pltpu.SemaphoreType.DMA((n_slots,)),
        ],
    )
    return pl.pallas_call(
        body,
        grid_spec=grid_spec,
        out_shape=jax.ShapeDtypeStruct((nh, n, 1, LANES), BF16),
        compiler_params=pltpu.CompilerParams(
            dimension_semantics=("arbitrary",), vmem_limit_bytes=VMEM_LIMIT_BYTES),
        name="attn_decode",
    )(pt_flat, *lams, gsub, expand, q, kn, vn, ck, cv)


def _finish_body(x_ref, a_ref, gbc_ref, wout_ref, gmlp_ref, wup_ref, wdown_ref, y_ref, *, fc):
    nh = a_ref.shape[0]
    mix = jnp.concatenate([a_ref[h] for h in range(nh)] + [gbc_ref[...]], axis=1)
    h = x_ref[...] + jnp.dot(mix, wout_ref[...], preferred_element_type=F32)
    ms = jnp.mean(h * h, axis=-1, keepdims=True)
    hn = (h * lax.rsqrt(ms + EPS) * gmlp_ref[...]).astype(BF16)
    y = h
    dff = wup_ref.shape[1]
    for c in range(dff // fc):
        z = jnp.dot(hn, wup_ref[:, c * fc:(c + 1) * fc], preferred_element_type=F32)
        r = jnp.maximum(z, 0.0)
        y = y + jnp.dot((r * r).astype(BF16), wdown_ref[c * fc:(c + 1) * fc, :],
                        preferred_element_type=F32)
    y_ref[...] = y


def _finish(x, a, gbc, wout, gmlp, wup, wdown, *, tm):
    b, s, d = x.shape
    nh = a.shape[1]
    cw = gbc.shape[2]
    dff = wup.shape[1]
    body = functools.partial(_finish_body, fc=_pick(dff, 1024))
    return pl.pallas_call(
        body,
        grid=(b, s // tm),
        in_specs=[
            pl.BlockSpec((None, tm, d), lambda i, j: (i, j, 0)),
            pl.BlockSpec((None, nh, tm, LANES), lambda i, j: (i, 0, j, 0)),
            pl.BlockSpec((None, tm, cw), lambda i, j: (i, j, 0)),
            _const_spec((d, d), 2),
            _const_spec((1, d), 2),
            _const_spec((d, dff), 2),
            _const_spec((dff, d), 2),
        ],
        out_specs=pl.BlockSpec((None, tm, d), lambda i, j: (i, j, 0)),
        out_shape=jax.ShapeDtypeStruct((b, s, d), F32),
        compiler_params=pltpu.CompilerParams(
            dimension_semantics=("arbitrary", "arbitrary"),
            vmem_limit_bytes=VMEM_LIMIT_BYTES),
        name="finish",
    )(x, a, gbc, wout, gmlp, wup, wdown)


def _rope_tables(pos, half):
    inv = ROPE_THETA ** (-jnp.arange(half, dtype=F32) / half)
    ang = pos.astype(F32)[:, None] * inv[None, :]
    cos = jnp.cos(ang)
    sin = jnp.sin(ang)
    return (jnp.concatenate([cos, cos, cos, cos], axis=1),
            jnp.concatenate([-sin, sin, -sin, sin], axis=1))


def kernel(x_prompt, x_sample, cache_k, cache_v, state_conv, page_table, g_attn_norm, w_in, g_q,
           g_k, lam_q1, lam_k1, lam_q2, lam_k2, g_subln, w_conv, w_out, g_mlp_norm, w_up, w_down):
    b, s, d = x_prompt.shape
    db, dec_seq, _ = x_sample.shape
    assert dec_seq == 1, "decode path handles one new token per sample"
    depth = w_in.shape[0]
    hd = g_q.shape[-1]
    vd = g_subln.shape[-1]
    assert vd == LANES and 2 * hd == LANES, "one head must fill one 128-lane block"
    aw = d // 2
    cw = d - aw
    nh = aw // vd
    n_pool = cache_k.shape[1]
    n_pages = page_table.shape[1]
    past = n_pages * PAGE_SIZE
    assert cache_k.shape[2] == PAGE_SIZE

    tm = _pick(s, 512)
    assert s % 2 == 0
    tk = _pick(s // 2, 512)
    tq = 2 * tk
    assert n_pages % 2 == 0
    cp = _pick(n_pages // 2, 16)

    cos_p, sin_p = _rope_tables(jnp.arange(s), hd // 2)
    cos_s, sin_s = _rope_tables(past + jnp.arange(1), hd // 2)
    grp = jnp.arange(aw) // hd
    gmat = ((grp[:, None] == grp[None, :]).astype(F32) / hd).astype(BF16)

    ck = jnp.transpose(cache_k, (0, 1, 3, 4, 5, 2)).reshape(depth * n_pool, aw, PAGE_SIZE)
    cv = cache_v.reshape(depth * n_pool, PAGE_SIZE * nh, vd)
    slot_of = jnp.arange(PAGE_SIZE * nh) // nh
    expand = (jnp.arange(PAGE_SIZE)[:, None] == slot_of[None, :]).astype(BF16)

    xp = x_prompt
    xs = x_sample.reshape(db, d)
    kp, vp, cpr, ksl, vsl, csl = [], [], [], [], [], []
    for l in range(depth):
        lam_init = 0.8 - 0.6 * math.exp(-0.3 * l)
        lams = tuple(v[l].reshape(1, hd).astype(F32) for v in (lam_q1, lam_k1, lam_q2, lam_k2))
        gat = g_attn_norm[l].reshape(1, d)
        gq = jnp.tile(g_q[l], aw // hd).reshape(1, aw)
        gk = jnp.tile(g_k[l], aw // hd).reshape(1, aw)
        gsub = g_subln[l].reshape(1, vd)
        gmlp = g_mlp_norm[l].reshape(1, d)
        win = w_in[l].astype(BF16)
        wout = w_out[l].astype(BF16)
        wup = w_up[l].astype(BF16)
        wdown = w_down[l].astype(BF16)
        wconv = w_conv[l]

        qt, k, v, kb, vt, gbc, convp = _inproj_prompt(xp, cos_p, sin_p, gat, gq, gk, win, gmat,
                                                      wconv, tm=tm)
        a = _attn_prompt(lams, gsub.reshape(vd, 1), qt, kb, vt, tq=tq, tk=tk, lam_init=lam_init)
        xp = _finish(xp, a, gbc, wout, gmlp, wup, wdown, tm=tm)
        kp.append(k.reshape(b, s, nh, 2, hd))
        vp.append(v.reshape(b, s, nh, vd))
        cpr.append(convp)

        qd, kn, vn, gbcd, convs = _inproj_decode(xs, cos_s, sin_s, gat, gq, gk, win, gmat, wconv,
                                                 state_conv[l].reshape(db, 2 * cw))
        pt_flat = (page_table + l * n_pool).reshape(-1).astype(jnp.int32)
        ad = _attn_decode(pt_flat, lams, gsub, expand, qd.reshape(db, 1, aw), kn.reshape(db, 1, aw),
                          vn.reshape(db, 1, aw), ck, cv, n_pages=n_pages, cp=cp,
                          lam_init=lam_init)
        xs = _finish(xs.reshape(1, db, d), ad.reshape(1, nh, db, LANES), gbcd.reshape(1, db, cw),
                     wout, gmlp, wup, wdown, tm=db).reshape(db, d)
        ksl.append(kn.reshape(db, 1, nh, 2, hd))
        vsl.append(vn.reshape(db, 1, nh, vd))
        csl.append(convs.reshape(db, 2, cw))

    return (xp, xs.reshape(db, 1, d), jnp.stack(kp), jnp.stack(vp), jnp.stack(cpr),
            jnp.stack(ksl), jnp.stack(vsl), jnp.stack(csl))
```

```python
import functools
import math

import jax
import jax.numpy as jnp
from jax import lax
from jax.experimental import pallas as pl
from jax.experimental.pallas import tpu as pltpu

F32 = jnp.float32
BF16 = jnp.bfloat16

PAGE_SIZE = 128
ROPE_THETA = 10000.0
EPS = 1e-6
NEG = -1e30

LANES = 128
SUBLANES = 8
VMEM_LIMIT_BYTES = 56 * 1024 * 1024


def _pick(n, target):
    t = min(n, target)
    while n % t:
        t -= 1
    return t


def _const_spec(shape, n_grid):
    zeros = (0,) * len(shape)
    if n_grid == 1:
        return pl.BlockSpec(shape, lambda a: zeros, pipeline_mode=pl.Buffered(1))
    if n_grid == 2:
        return pl.BlockSpec(shape, lambda a, b: zeros, pipeline_mode=pl.Buffered(1))
    return pl.BlockSpec(shape, lambda a, b, c: zeros, pipeline_mode=pl.Buffered(1))


def _lam_value(lq1, lk1, lq2, lk2, lam_init):
    s1 = jnp.sum(lq1 * lk1, axis=-1, keepdims=True)
    s2 = jnp.sum(lq2 * lk2, axis=-1, keepdims=True)
    return jnp.exp(s1) - jnp.exp(s2) + lam_init


def _group_rms_rope(z, gain, gmat, cos, sin):
    zz = z * z
    hi = zz.astype(BF16)
    lo = (zz - hi.astype(F32)).astype(BF16)
    g = gmat
    ms = (jnp.dot(hi, g, preferred_element_type=F32)
          + jnp.dot(lo, g, preferred_element_type=F32))
    zn = z * lax.rsqrt(ms + EPS) * gain
    t = z.shape[0]
    lane = lax.broadcasted_iota(jnp.int32, (t, LANES), 1)
    upper = (lane & 32) != 0
    outs = []
    for c in range(z.shape[1] // LANES):
        blk = zn[:, c * LANES:(c + 1) * LANES]
        swapped = jnp.where(upper, pltpu.roll(blk, 32, 1), pltpu.roll(blk, LANES - 32, 1))
        outs.append(blk * cos + swapped * sin)
    return outs


def _project(x, gat, win, gq, gk, gmat, cos, sin, aw):
    ms = jnp.mean(x * x, axis=-1, keepdims=True)
    n = (x * lax.rsqrt(ms + EPS) * gat).astype(BF16)
    z = jnp.dot(n, win, preferred_element_type=F32)
    q_blocks = _group_rms_rope(z[:, 0:aw], gq, gmat, cos, sin)
    k_blocks = _group_rms_rope(z[:, aw:2 * aw], gk, gmat, cos, sin)
    v = z[:, 2 * aw:3 * aw]
    cw = (z.shape[1] - 3 * aw) // 3
    gb = z[:, 3 * aw:3 * aw + cw]
    gc = z[:, 3 * aw + cw:3 * aw + 2 * cw]
    h = z[:, 3 * aw + 2 * cw:]
    return q_blocks, k_blocks, v, gb, gc * h


def _inproj_prompt_body(x_ref, cos_ref, sin_ref, gat_ref, gq_ref, gk_ref, win_ref, gmat_ref,
                        wconv_ref, qt_ref, k_ref, v_ref, kb_ref, vt_ref, gbc_ref, convp_ref,
                        ubuf, *, tm, aw, scale):
    j = pl.program_id(1)
    nh = aw // LANES
    q_blocks, k_blocks, v, gb, u = _project(
        x_ref[...], gat_ref[...], win_ref[...], gq_ref[...], gk_ref[...], gmat_ref[...],
        cos_ref[...], sin_ref[...], aw)
    for h, (qb, kb) in enumerate(zip(q_blocks, k_blocks)):
        vh = v[:, h * LANES:(h + 1) * LANES]
        qt_ref[h] = (qb * scale).T.astype(BF16)
        vt_ref[h] = vh.T.astype(BF16)
        k_ref[:, h * LANES:(h + 1) * LANES] = kb
        kb_ref[h] = kb.astype(BF16)
        v_ref[pl.ds(h, tm, stride=nh), :] = vh

    @pl.when(j == 0)
    def _():
        ubuf[0:SUBLANES, :] = jnp.zeros((SUBLANES, ubuf.shape[1]), F32)

    ubuf[SUBLANES:SUBLANES + tm, :] = u
    w = wconv_ref[...]
    c = (ubuf[SUBLANES - 2:SUBLANES - 2 + tm, :] * w[0:1, :]
         + ubuf[SUBLANES - 1:SUBLANES - 1 + tm, :] * w[1:2, :]
         + u * w[2:3, :])
    gbc_ref[...] = (gb * c).astype(BF16)
    convp_ref[...] = ubuf[SUBLANES + tm - 2:SUBLANES + tm, :]
    ubuf[0:SUBLANES, :] = ubuf[tm:tm + SUBLANES, :]


def _inproj_prompt(x, cos, sin, gat, gq, gk, win, gmat, wconv, *, tm):
    b, s, d = x.shape
    aw = gmat.shape[0]
    cw = wconv.shape[1]
    nh = aw // LANES
    pw = win.shape[1]
    grid = (b, s // tm)
    body = functools.partial(_inproj_prompt_body, tm=tm, aw=aw,
                             scale=(LANES // 2) ** -0.5 * math.log2(math.e))
    return pl.pallas_call(
        body,
        grid=grid,
        in_specs=[
            pl.BlockSpec((None, tm, d), lambda i, j: (i, j, 0)),
            pl.BlockSpec((tm, LANES), lambda i, j: (j, 0)),
            pl.BlockSpec((tm, LANES), lambda i, j: (j, 0)),
            _const_spec((1, d), 2),
            _const_spec((1, aw), 2),
            _const_spec((1, aw), 2),
            _const_spec((d, pw), 2),
            _const_spec((aw, aw), 2),
            _const_spec((3, cw), 2),
        ],
        out_specs=[
            pl.BlockSpec((None, nh, LANES, tm), lambda i, j: (i, 0, 0, j)),
            pl.BlockSpec((None, tm, aw), lambda i, j: (i, j, 0)),
            pl.BlockSpec((None, tm * nh, LANES), lambda i, j: (i, j, 0)),
            pl.BlockSpec((None, nh, tm, LANES), lambda i, j: (i, 0, j, 0)),
            pl.BlockSpec((None, nh, LANES, tm), lambda i, j: (i, 0, 0, j)),
            pl.BlockSpec((None, tm, cw), lambda i, j: (i, j, 0)),
            pl.BlockSpec((None, 2, cw), lambda i, j: (i, 0, 0)),
        ],
        out_shape=[
            jax.ShapeDtypeStruct((b, nh, LANES, s), BF16),
            jax.ShapeDtypeStruct((b, s, aw), F32),
            jax.ShapeDtypeStruct((b, s * nh, LANES), F32),
            jax.ShapeDtypeStruct((b, nh, s, LANES), BF16),
            jax.ShapeDtypeStruct((b, nh, LANES, s), BF16),
            jax.ShapeDtypeStruct((b, s, cw), BF16),
            jax.ShapeDtypeStruct((b, 2, cw), F32),
        ],
        scratch_shapes=[pltpu.VMEM((tm + SUBLANES, cw), F32)],
        compiler_params=pltpu.CompilerParams(
            dimension_semantics=("arbitrary", "arbitrary"),
            vmem_limit_bytes=VMEM_LIMIT_BYTES),
        name="inproj_prompt",
    )(x, cos, sin, gat, gq, gk, win, gmat, wconv)


def _inproj_decode_body(x_ref, cos_ref, sin_ref, gat_ref, gq_ref, gk_ref, win_ref, gmat_ref,
                        wconv_ref, st_ref, q_ref, k_ref, v_ref, gbc_ref, convs_ref, *, aw, scale):
    q_blocks, k_blocks, v, gb, u = _project(
        x_ref[...], gat_ref[...], win_ref[...], gq_ref[...], gk_ref[...], gmat_ref[...],
        cos_ref[...], sin_ref[...], aw)
    for h, (qb, kb) in enumerate(zip(q_blocks, k_blocks)):
        q_ref[:, h * LANES:(h + 1) * LANES] = (qb * scale).astype(BF16)
        k_ref[:, h * LANES:(h + 1) * LANES] = kb
    v_ref[...] = v
    cw = u.shape[1]
    w = wconv_ref[...]
    s0 = st_ref[:, 0:cw]
    s1 = st_ref[:, cw:2 * cw]
    c = s0 * w[0:1, :] + s1 * w[1:2, :] + u * w[2:3, :]
    gbc_ref[...] = (gb * c).astype(BF16)
    convs_ref[:, 0:cw] = s1
    convs_ref[:, cw:2 * cw] = u


def _inproj_decode(x, cos, sin, gat, gq, gk, win, gmat, wconv, state):
    n, d = x.shape
    aw = gmat.shape[0]
    cw = wconv.shape[1]
    nh = aw // LANES
    body = functools.partial(_inproj_decode_body, aw=aw, scale=(LANES // 2) ** -0.5)
    full = lambda shape: pl.BlockSpec(shape, lambda i: (0,) * len(shape))
    return pl.pallas_call(
        body,
        grid=(1,),
        in_specs=[full((n, d)), full((1, LANES)), full((1, LANES)), full((1, d)), full((1, aw)),
                  full((1, aw)), full(win.shape), full((aw, aw)), full((3, cw)), full((n, 2 * cw))],
        out_specs=[full((n, aw)), full((n, aw)), full((n, aw)), full((n, cw)), full((n, 2 * cw))],
        out_shape=[
            jax.ShapeDtypeStruct((n, aw), BF16),
            jax.ShapeDtypeStruct((n, aw), F32),
            jax.ShapeDtypeStruct((n, aw), F32),
            jax.ShapeDtypeStruct((n, cw), BF16),
            jax.ShapeDtypeStruct((n, 2 * cw), F32),
        ],
        compiler_params=pltpu.CompilerParams(
            dimension_semantics=("arbitrary",), vmem_limit_bytes=VMEM_LIMIT_BYTES),
        name="inproj_decode",
    )(x, cos, sin, gat, gq, gk, win, gmat, wconv, state)


def _subln(o, gsub, post_scale):
    ms = jnp.mean(o * o, axis=-1, keepdims=True)
    return o * lax.rsqrt(ms + EPS) * gsub * post_scale


def _attn_body(pt_ref, lq1_ref, lk1_ref, lq2_ref, lk2_ref, gcol_ref, grow_ref, exp_ref,
               qt_ref, k_ref, vt_ref, qd_ref, kn_ref, vn_ref, ck_hbm, cv_hbm,
               o_ref, od_ref,
               qz_scr, s_scr, mb_scr, m_scr, l_scr, acc_scr, kbuf, vbuf, ksem, vsem, dstate,
               *, tq, tk, n_pages, cp, spp, lam_init):
    assert tq == 2 * tk
    i = pl.program_id(2)
    step = (pl.program_id(0) * pl.num_programs(1) + pl.program_id(1)) * pl.num_programs(2) + i
    n_steps = pl.num_programs(0) * pl.num_programs(1) * pl.num_programs(2)
    nc = n_pages // cp
    per_step = spp * nc
    nh = od_ref.shape[0]
    lam = _lam_value(lq1_ref[...], lk1_ref[...], lq2_ref[...], lk2_ref[...], lam_init)

    def copies(at_step, pos, slot):
        first_page = (at_step * spp + pos // nc) * n_pages + (pos % nc) * cp
        out = []
        for p in range(cp):
            page = pt_ref[first_page + p]
            out.append(pltpu.make_async_copy(ck_hbm.at[page], kbuf.at[slot, p], ksem.at[slot]))
            out.append(pltpu.make_async_copy(cv_hbm.at[page], vbuf.at[slot, p], vsem.at[slot]))
        return out

    def start(at_step, pos):
        for cpy in copies(at_step, pos, pos % 2):
            cpy.start()

    def decode_chunk(pos):
        t, c, slot = pos // nc, pos % nc, pos % 2
        for cpy in copies(step, pos, slot):
            cpy.wait()
        q = qd_ref[t]
        if c == 0:
            state = _decode_new_token(q, kn_ref[t], vn_ref[t])
        else:
            state = (dstate[0], dstate[1], dstate[2])
        state = _decode_chunk(q, kbuf[slot], vbuf[slot], exp_ref[...], state)
        if c == nc - 1:
            a = _decode_output(state, lam, grow_ref[...], lam_init)
            for h in range(nh):
                od_ref[h, t] = a[2 * h:2 * h + 1, :].astype(BF16)
        else:
            for n, v in enumerate(state):
                dstate[n] = v
        if pos + 2 < per_step:
            start(step, pos + 2)
        else:
            @pl.when(step + 1 < n_steps)
            def _():
                start(step + 1, pos + 2 - per_step)

    def decode_point(point):
        for pos in range(per_step):
            if (pos * 4) // per_step == point:
                decode_chunk(pos)

    @pl.when(step == 0)
    def _():
        start(0, 0)
        start(0, 1)

    decode_point(0)
    n2 = 2 * tq
    qt = qt_ref[...]
    sub = lax.broadcasted_iota(jnp.int32, qt.shape, 0)
    zero = jnp.zeros_like(qt)
    qz_scr[:, 0:tq] = jnp.where(sub < LANES // 2, qt, zero)
    qz_scr[:, tq:n2] = jnp.where(sub >= LANES // 2, qt, zero)
    m_scr[...] = jnp.full(m_scr.shape, NEG, F32)
    l_scr[...] = jnp.zeros(l_scr.shape, F32)
    acc_scr[...] = jnp.zeros(acc_scr.shape, F32)

    def stage_a(start, buf, diag_offset):
        kblk = k_ref[pl.ds(pl.multiple_of(start, tk), tk), :]
        s = jnp.dot(kblk, qz_scr[...], preferred_element_type=F32)
        if diag_offset is not None:
            key = lax.broadcasted_iota(jnp.int32, s.shape, 0) + diag_offset
            col = lax.broadcasted_iota(jnp.int32, s.shape, 1)
            qry = jnp.where(col >= tq, col - tq, col)
            s = jnp.where(key <= qry, s, NEG)
        s_scr[buf] = s
        mb_scr[buf] = jnp.max(s, axis=0, keepdims=True)

    def stage_b(start, buf):
        m_prev = m_scr[...]
        m_new = jnp.maximum(m_prev, mb_scr[buf])
        alpha = jnp.exp2(m_prev - m_new)
        p = jnp.exp2(s_scr[buf] - m_new)
        l_scr[...] = alpha * l_scr[...] + jnp.sum(p, axis=0, keepdims=True)
        vt = vt_ref[:, pl.ds(pl.multiple_of(start, tk), tk)]
        acc_scr[...] = alpha * acc_scr[...] + jnp.dot(vt, p.astype(BF16),
                                                      preferred_element_type=F32)
        m_scr[...] = m_new

    diag0 = i * tq
    stage_a(diag0, 0, 0)
    decode_point(1)

    def pair(u, carry):
        first = 2 * u * tk
        stage_a(first, 1, None)
        stage_b(jnp.where(u == 0, diag0, first - tk), 0)
        stage_a(first + tk, 0, None)
        stage_b(first, 1)
        return carry

    lax.fori_loop(0, i, pair, 0)
    decode_point(2)
    stage_a(diag0 + tk, 1, tk)
    stage_b(jnp.where(i == 0, diag0, diag0 - tk), 0)
    stage_b(diag0 + tk, 1)
    decode_point(3)

    ot = (acc_scr[:, 0:tq] / l_scr[:, 0:tq]
          - lam * (acc_scr[:, tq:n2] / l_scr[:, tq:n2]))
    ms = jnp.mean(ot * ot, axis=0, keepdims=True)
    at = ot * lax.rsqrt(ms + EPS) * gcol_ref[...] * (1.0 - lam_init)
    o_ref[...] = at.T.astype(BF16)


def _attention(pt_flat, lams, gsub, expand, qt, k, vt, qd, kn, vn, ck, cv,
               *, tq, tk, n_pages, cp, lam_init):
    b, nh, s, _ = k.shape
    n, _, aw = qd.shape
    hd = lams[0].shape[1]
    nq = s // tq
    n_steps = b * nh * nq
    assert n % n_steps == 0, "decode samples must split evenly over the attention grid steps"
    spp = n // n_steps
    body = functools.partial(_attn_body, tq=tq, tk=tk, n_pages=n_pages, cp=cp, spp=spp,
                             lam_init=lam_init)
    const = lambda shape: pl.BlockSpec(shape, lambda bi, h, i, pt: (0,) * len(shape))
    sample_rows = pl.BlockSpec((spp, 1, aw), lambda bi, h, i, pt: ((bi * nh + h) * nq + i, 0, 0))
    grid_spec = pltpu.PrefetchScalarGridSpec(
        num_scalar_prefetch=1,
        grid=(b, nh, nq),
        in_specs=[const((1, hd))] * 4 + [
            const((LANES, 1)), const((1, LANES)), const(expand.shape),
            pl.BlockSpec((None, None, LANES, tq), lambda bi, h, i, pt: (bi, h, 0, i)),
            pl.BlockSpec((None, None, s, LANES), lambda bi, h, i, pt: (bi, h, 0, 0)),
            pl.BlockSpec((None, None, LANES, s), lambda bi, h, i, pt: (bi, h, 0, 0)),
            sample_rows, sample_rows, sample_rows,
            pl.BlockSpec(memory_space=pl.ANY),
            pl.BlockSpec(memory_space=pl.ANY),
        ],
        out_specs=[
            pl.BlockSpec((None, None, tq, LANES), lambda bi, h, i, pt: (bi, h, i, 0)),
            pl.BlockSpec((nh, spp, 1, LANES), lambda bi, h, i, pt: (0, (bi * nh + h) * nq + i, 0, 0)),
        ],
        scratch_shapes=[
            pltpu.VMEM((LANES, 2 * tq), BF16),
            pltpu.VMEM((2, tk, 2 * tq), F32),
            pltpu.VMEM((2, 1, 2 * tq), F32),
            pltpu.VMEM((1, 2 * tq), F32),
            pltpu.VMEM((1, 2 * tq), F32),
            pltpu.VMEM((LANES, 2 * tq), F32),
            pltpu.VMEM((2, cp) + ck.shape[1:], F32),
            pltpu.VMEM((2, cp) + cv.shape[1:], F32),
            pltpu.SemaphoreType.DMA((2,)),
            pltpu.SemaphoreType.DMA((2,)),
            pltpu.VMEM((3, SUBLANES, LANES), F32),
        ],
    )
    return pl.pallas_call(
        body,
        grid_spec=grid_spec,
        out_shape=[jax.ShapeDtypeStruct((b, nh, s, LANES), BF16),
                   jax.ShapeDtypeStruct((nh, n, 1, LANES), BF16)],
        compiler_params=pltpu.CompilerParams(
            dimension_semantics=("arbitrary", "arbitrary", "arbitrary"),
            vmem_limit_bytes=VMEM_LIMIT_BYTES),
        name="attention",
    )(pt_flat, *lams, gsub.reshape(LANES, 1), gsub, expand, qt, k, vt, qd, kn, vn, ck, cv)


def _decode_query_rows(q):
    aw = q.shape[-1]
    row = lax.broadcasted_iota(jnp.int32, (SUBLANES, aw), 0)
    lane = lax.broadcasted_iota(jnp.int32, (SUBLANES, aw), 1)
    return jnp.where(lane // (LANES // 2) == row, q.astype(F32), 0.0)


def _decode_new_token(q, kn, vn):
    nh = q.shape[-1] // LANES
    m = jnp.sum(_decode_query_rows(q) * kn, axis=1, keepdims=True)
    m = jnp.broadcast_to(m, (SUBLANES, LANES))
    l = jnp.ones((SUBLANES, LANES), F32)
    row_v = lax.broadcasted_iota(jnp.int32, (SUBLANES, LANES), 0)
    acc = jnp.zeros((SUBLANES, LANES), F32)
    for h in range(nh):
        acc = jnp.where(row_v // 2 == h, vn[:, h * LANES:(h + 1) * LANES], acc)
    return m, l, acc


def _decode_chunk(q, kpages, vpages, expand, state):
    m, l, acc = state
    cp = kpages.shape[0]
    nh = q.shape[-1] // LANES
    qbd_b = _decode_query_rows(q).astype(BF16)
    s = [jnp.dot(qbd_b, kpages[p].astype(BF16), preferred_element_type=F32)
         for p in range(cp)]
    s_max = functools.reduce(jnp.maximum, s)
    m_new = jnp.maximum(m, jnp.max(s_max, axis=1, keepdims=True))
    alpha = jnp.exp(m - m_new)
    p_all = jnp.concatenate([jnp.exp(sp - m_new) for sp in s], axis=0)
    l = alpha * l + jnp.sum(
        jnp.sum(p_all.reshape(cp, SUBLANES, LANES), axis=0), axis=1, keepdims=True)
    w_all = jnp.dot(p_all.astype(BF16), expand, preferred_element_type=F32)
    row_e = lax.broadcasted_iota(jnp.int32, (SUBLANES, nh * PAGE_SIZE), 0)
    lane_e = lax.broadcasted_iota(jnp.int32, (SUBLANES, nh * PAGE_SIZE), 1)
    own_head = lane_e % nh == row_e // 2
    pv = jnp.zeros((SUBLANES, LANES), F32)
    for p in range(cp):
        w_p = jnp.where(own_head, w_all[p * SUBLANES:(p + 1) * SUBLANES, :], 0.0).astype(BF16)
        pv = pv + jnp.dot(w_p, vpages[p].astype(BF16), preferred_element_type=F32)
    return m_new, l, alpha * acc + pv


def _decode_output(state, lam, gsub_row, lam_init):
    _, l, acc = state
    row_v = lax.broadcasted_iota(jnp.int32, (SUBLANES, LANES), 0)
    t = (acc / l) * jnp.where(row_v % 2 == 0, 1.0, -lam)
    o = t + pltpu.roll(t, SUBLANES - 1, 0)
    return _subln(o, gsub_row, 1.0 - lam_init)


def _finish_body(x_ref, a_ref, gbc_ref, wout_ref, gmlp_ref, wup_ref, wdown_ref, y_ref, *, fc):
    nh = a_ref.shape[0]
    mix = jnp.concatenate([a_ref[h] for h in range(nh)] + [gbc_ref[...]], axis=1)
    h = x_ref[...] + jnp.dot(mix, wout_ref[...], preferred_element_type=F32)
    ms = jnp.mean(h * h, axis=-1, keepdims=True)
    hn = (h * lax.rsqrt(ms + EPS) * gmlp_ref[...]).astype(BF16)
    y = h
    dff = wup_ref.shape[1]
    for c in range(dff // fc):
        z = jnp.dot(hn, wup_ref[:, c * fc:(c + 1) * fc], preferred_element_type=F32)
        r = jnp.maximum(z, 0.0)
        y = y + jnp.dot((r * r).astype(BF16), wdown_ref[c * fc:(c + 1) * fc, :],
                        preferred_element_type=F32)
    y_ref[...] = y


def _finish(x, a, gbc, wout, gmlp, wup, wdown, *, tm):
    b, s, d = x.shape
    nh = a.shape[1]
    cw = gbc.shape[2]
    dff = wup.shape[1]
    body = functools.partial(_finish_body, fc=_pick(dff, 1024))
    return pl.pallas_call(
        body,
        grid=(b, s // tm),
        in_specs=[
            pl.BlockSpec((None, tm, d), lambda i, j: (i, j, 0)),
            pl.BlockSpec((None, nh, tm, LANES), lambda i, j: (i, 0, j, 0)),
            pl.BlockSpec((None, tm, cw), lambda i, j: (i, j, 0)),
            _const_spec((d, d), 2),
            _const_spec((1, d), 2),
            _const_spec((d, dff), 2),
            _const_spec((dff, d), 2),
        ],
        out_specs=pl.BlockSpec((None, tm, d), lambda i, j: (i, j, 0)),
        out_shape=jax.ShapeDtypeStruct((b, s, d), F32),
        compiler_params=pltpu.CompilerParams(
            dimension_semantics=("arbitrary", "arbitrary"),
            vmem_limit_bytes=VMEM_LIMIT_BYTES),
        name="finish",
    )(x, a, gbc, wout, gmlp, wup, wdown)


def _rope_tables(pos, half):
    inv = ROPE_THETA ** (-jnp.arange(half, dtype=F32) / half)
    ang = pos.astype(F32)[:, None] * inv[None, :]
    cos = jnp.cos(ang)
    sin = jnp.sin(ang)
    return (jnp.concatenate([cos, cos, cos, cos], axis=1),
            jnp.concatenate([-sin, sin, -sin, sin], axis=1))


def kernel(x_prompt, x_sample, cache_k, cache_v, state_conv, page_table, g_attn_norm, w_in, g_q,
           g_k, lam_q1, lam_k1, lam_q2, lam_k2, g_subln, w_conv, w_out, g_mlp_norm, w_up, w_down):
    b, s, d = x_prompt.shape
    db, dec_seq, _ = x_sample.shape
    assert dec_seq == 1, "decode path handles one new token per sample"
    depth = w_in.shape[0]
    hd = g_q.shape[-1]
    vd = g_subln.shape[-1]
    assert vd == LANES and 2 * hd == LANES, "one head must fill one 128-lane block"
    aw = d // 2
    cw = d - aw
    nh = aw // vd
    n_pool = cache_k.shape[1]
    n_pages = page_table.shape[1]
    past = n_pages * PAGE_SIZE
    assert cache_k.shape[2] == PAGE_SIZE

    tm = _pick(s, 512)
    assert s % 2 == 0
    tk = _pick(s // 2, 512)
    tq = 2 * tk
    assert n_pages % 2 == 0
    cp = _pick(n_pages // 2, 16)

    cos_p, sin_p = _rope_tables(jnp.arange(s), hd // 2)
    cos_s, sin_s = _rope_tables(past + jnp.arange(1), hd // 2)
    grp = jnp.arange(aw) // hd
    gmat = ((grp[:, None] == grp[None, :]).astype(F32) / hd).astype(BF16)

    ck = jnp.transpose(cache_k, (0, 1, 3, 4, 5, 2)).reshape(depth * n_pool, aw, PAGE_SIZE)
    cv = cache_v.reshape(depth * n_pool, PAGE_SIZE * nh, vd)
    slot_of = jnp.arange(PAGE_SIZE * nh) // nh
    expand = (jnp.arange(PAGE_SIZE)[:, None] == slot_of[None, :]).astype(BF16)

    xp = x_prompt
    xs = x_sample.reshape(db, d)
    kp, vp, cpr, ksl, vsl, csl = [], [], [], [], [], []
    for l in range(depth):
        lam_init = 0.8 - 0.6 * math.exp(-0.3 * l)
        lams = tuple(v[l].reshape(1, hd).astype(F32) for v in (lam_q1, lam_k1, lam_q2, lam_k2))
        gat = g_attn_norm[l].reshape(1, d)
        gq = jnp.tile(g_q[l], aw // hd).reshape(1, aw)
        gk = jnp.tile(g_k[l], aw // hd).reshape(1, aw)
        gsub = g_subln[l].reshape(1, vd)
        gmlp = g_mlp_norm[l].reshape(1, d)
        win = w_in[l].astype(BF16)
        wout = w_out[l].astype(BF16)
        wup = w_up[l].astype(BF16)
        wdown = w_down[l].astype(BF16)
        wconv = w_conv[l]

        qt, k, v, kb, vt, gbc, convp = _inproj_prompt(xp, cos_p, sin_p, gat, gq, gk, win, gmat,
                                                      wconv, tm=tm)
        qd, kn, vn, gbcd, convs = _inproj_decode(xs, cos_s, sin_s, gat, gq, gk, win, gmat, wconv,
                                                 state_conv[l].reshape(db, 2 * cw))
        pt_flat = (page_table + l * n_pool).reshape(-1).astype(jnp.int32)
        a, ad = _attention(pt_flat, lams, gsub, expand, qt, kb, vt, qd.reshape(db, 1, aw),
                           kn.reshape(db, 1, aw), vn.reshape(db, 1, aw), ck, cv,
                           tq=tq, tk=tk, n_pages=n_pages, cp=cp, lam_init=lam_init)
        xp = _finish(xp, a, gbc, wout, gmlp, wup, wdown, tm=tm)
        kp.append(k.reshape(b, s, nh, 2, hd))
        vp.append(v.reshape(b, s, nh, vd))
        cpr.append(convp)
        xs = _finish(xs.reshape(1, db, d), ad.reshape(1, nh, db, LANES), gbcd.reshape(1, db, cw),
                     wout, gmlp, wup, wdown, tm=db).reshape(db, d)
        ksl.append(kn.reshape(db, 1, nh, 2, hd))
        vsl.append(vn.reshape(db, 1, nh, vd))
        csl.append(convs.reshape(db, 2, cw))

    return (xp, xs.reshape(db, 1, d), jnp.stack(kp), jnp.stack(vp), jnp.stack(cpr),
            jnp.stack(ksl), jnp.stack(vsl), jnp.stack(csl))
```

```python
import functools
import math

import jax
import jax.numpy as jnp
from jax import lax
from jax.experimental import pallas as pl
from jax.experimental.pallas import tpu as pltpu

F32 = jnp.float32
BF16 = jnp.bfloat16

PAGE_SIZE = 128
ROPE_THETA = 10000.0
EPS = 1e-6
NEG = -1e30

LANES = 128
SUBLANES = 8
VMEM_LIMIT_BYTES = 60 * 1024 * 1024


def _pick(n, target):
    t = min(n, target)
    while n % t:
        t -= 1
    return t


def _const_spec(shape, n_grid):
    zeros = (0,) * len(shape)
    if n_grid == 1:
        return pl.BlockSpec(shape, lambda a: zeros, pipeline_mode=pl.Buffered(1))
    if n_grid == 2:
        return pl.BlockSpec(shape, lambda a, b: zeros, pipeline_mode=pl.Buffered(1))
    return pl.BlockSpec(shape, lambda a, b, c: zeros, pipeline_mode=pl.Buffered(1))


def _lam_value(lq1, lk1, lq2, lk2, lam_init):
    s1 = jnp.sum(lq1 * lk1, axis=-1, keepdims=True)
    s2 = jnp.sum(lq2 * lk2, axis=-1, keepdims=True)
    return jnp.exp(s1) - jnp.exp(s2) + lam_init


def _group_rms_rope(z, gain, gmat, cos, sin):
    zz = z * z
    hi = zz.astype(BF16)
    lo = (zz - hi.astype(F32)).astype(BF16)
    g = gmat
    ms = (jnp.dot(hi, g, preferred_element_type=F32)
          + jnp.dot(lo, g, preferred_element_type=F32))
    zn = z * lax.rsqrt(ms + EPS) * gain
    t = z.shape[0]
    lane = lax.broadcasted_iota(jnp.int32, (t, LANES), 1)
    upper = (lane & 32) != 0
    outs = []
    for c in range(z.shape[1] // LANES):
        blk = zn[:, c * LANES:(c + 1) * LANES]
        swapped = jnp.where(upper, pltpu.roll(blk, 32, 1), pltpu.roll(blk, LANES - 32, 1))
        outs.append(blk * cos + swapped * sin)
    return outs


def _project(x, gat, win, gq, gk, gmat, cos, sin, aw):
    ms = jnp.mean(x * x, axis=-1, keepdims=True)
    n = (x * lax.rsqrt(ms + EPS) * gat).astype(BF16)
    z = jnp.dot(n, win, preferred_element_type=F32)
    q_blocks = _group_rms_rope(z[:, 0:aw], gq, gmat, cos, sin)
    k_blocks = _group_rms_rope(z[:, aw:2 * aw], gk, gmat, cos, sin)
    v = z[:, 2 * aw:3 * aw]
    cw = (z.shape[1] - 3 * aw) // 3
    gb = z[:, 3 * aw:3 * aw + cw]
    gc = z[:, 3 * aw + cw:3 * aw + 2 * cw]
    h = z[:, 3 * aw + 2 * cw:]
    return q_blocks, k_blocks, v, gb, gc * h


def _head_norm_rope_t(zt, gain, cos, sin):
    hd = zt.shape[0] // 2
    rot = hd // 2
    t = zt.shape[1]
    halves = []
    for c in range(2):
        x = zt[c * hd:(c + 1) * hd, :]
        ms = jnp.mean(x * x, axis=0, keepdims=True)
        halves.append(x * lax.rsqrt(ms + EPS))
    zn = jnp.concatenate(halves, axis=0)
    zn = jnp.concatenate([zn[:, j * LANES:(j + 1) * LANES] * gain for j in range(t // LANES)],
                         axis=1)
    out = []
    for c in range(2):
        x1 = zn[c * hd:c * hd + rot, :]
        x2 = zn[c * hd + rot:(c + 1) * hd, :]
        out += [x1 * cos - x2 * sin, x2 * cos + x1 * sin]
    return jnp.concatenate(out, axis=0)


def _inproj_prompt_body(x_ref, cos_ref, sin_ref, gat_ref, gq_ref, gk_ref, win_ref,
                        wconv_ref, qt_ref, kt_ref, v_ref, kb_ref, vt_ref, gbc_ref, convp_ref,
                        ubuf, *, tm, aw, scale):
    j = pl.program_id(1)
    nh = aw // LANES
    x = x_ref[...]
    ms = jnp.mean(x * x, axis=-1, keepdims=True)
    n = (x * lax.rsqrt(ms + EPS) * gat_ref[...]).astype(BF16)
    z = jnp.dot(n, win_ref[...], preferred_element_type=F32)
    cw = (z.shape[1] - 3 * aw) // 3
    gb = z[:, 3 * aw:3 * aw + cw]
    u = z[:, 3 * aw + cw:3 * aw + 2 * cw] * z[:, 3 * aw + 2 * cw:]
    cos = cos_ref[...]
    sin = sin_ref[...]
    for h in range(nh):
        cols = slice(h * LANES, (h + 1) * LANES)
        qt = _head_norm_rope_t(z[:, cols].T, gq_ref[...], cos, sin)
        kt = _head_norm_rope_t(z[:, aw:2 * aw][:, cols].T, gk_ref[...], cos, sin)
        vh = z[:, 2 * aw:3 * aw][:, cols]
        qt_ref[h] = (qt * scale).astype(BF16)
        vt_ref[h] = vh.T.astype(BF16)
        kt_ref[cols, :] = kt
        kb_ref[h] = kt.T.astype(BF16)
        v_ref[pl.ds(h, tm, stride=nh), :] = vh

    @pl.when(j == 0)
    def _():
        ubuf[0:SUBLANES, :] = jnp.zeros((SUBLANES, ubuf.shape[1]), F32)

    ubuf[SUBLANES:SUBLANES + tm, :] = u
    w = wconv_ref[...]
    c = (ubuf[SUBLANES - 2:SUBLANES - 2 + tm, :] * w[0:1, :]
         + ubuf[SUBLANES - 1:SUBLANES - 1 + tm, :] * w[1:2, :]
         + u * w[2:3, :])
    gbc_ref[...] = (gb * c).astype(BF16)
    convp_ref[...] = ubuf[SUBLANES + tm - 2:SUBLANES + tm, :]
    ubuf[0:SUBLANES, :] = ubuf[tm:tm + SUBLANES, :]


def _inproj_prompt(x, cos, sin, gat, gq, gk, win, wconv, *, tm):
    b, s, d = x.shape
    cw = wconv.shape[1]
    pw = win.shape[1]
    aw = (pw - 3 * cw) // 3
    nh = aw // LANES
    rot = cos.shape[0]
    grid = (b, s // tm)
    body = functools.partial(_inproj_prompt_body, tm=tm, aw=aw,
                             scale=(LANES // 2) ** -0.5 * math.log2(math.e))
    return pl.pallas_call(
        body,
        grid=grid,
        in_specs=[
            pl.BlockSpec((None, tm, d), lambda i, j: (i, j, 0)),
            pl.BlockSpec((rot, tm), lambda i, j: (0, j)),
            pl.BlockSpec((rot, tm), lambda i, j: (0, j)),
            _const_spec((1, d), 2),
            _const_spec((LANES, LANES), 2),
            _const_spec((LANES, LANES), 2),
            _const_spec((d, pw), 2),
            _const_spec((3, cw), 2),
        ],
        out_specs=[
            pl.BlockSpec((None, nh, LANES, tm), lambda i, j: (i, 0, 0, j)),
            pl.BlockSpec((None, aw, tm), lambda i, j: (i, 0, j)),
            pl.BlockSpec((None, tm * nh, LANES), lambda i, j: (i, j, 0)),
            pl.BlockSpec((None, nh, tm, LANES), lambda i, j: (i, 0, j, 0)),
            pl.BlockSpec((None, nh, LANES, tm), lambda i, j: (i, 0, 0, j)),
            pl.BlockSpec((None, tm, cw), lambda i, j: (i, j, 0)),
            pl.BlockSpec((None, 2, cw), lambda i, j: (i, 0, 0)),
        ],
        out_shape=[
            jax.ShapeDtypeStruct((b, nh, LANES, s), BF16),
            jax.ShapeDtypeStruct((b, aw, s), F32),
            jax.ShapeDtypeStruct((b, s * nh, LANES), F32),
            jax.ShapeDtypeStruct((b, nh, s, LANES), BF16),
            jax.ShapeDtypeStruct((b, nh, LANES, s), BF16),
            jax.ShapeDtypeStruct((b, s, cw), BF16),
            jax.ShapeDtypeStruct((b, 2, cw), F32),
        ],
        scratch_shapes=[pltpu.VMEM((tm + SUBLANES, cw), F32)],
        compiler_params=pltpu.CompilerParams(
            dimension_semantics=("arbitrary", "arbitrary"),
            vmem_limit_bytes=VMEM_LIMIT_BYTES),
        name="inproj_prompt",
    )(x, cos, sin, gat, gq, gk, win, wconv)


def _inproj_decode_body(x_ref, cos_ref, sin_ref, gat_ref, gq_ref, gk_ref, win_ref, gmat_ref,
                        wconv_ref, st_ref, q_ref, k_ref, v_ref, gbc_ref, convs_ref, *, aw, scale):
    q_blocks, k_blocks, v, gb, u = _project(
        x_ref[...], gat_ref[...], win_ref[...], gq_ref[...], gk_ref[...], gmat_ref[...],
        cos_ref[...], sin_ref[...], aw)
    for h, (qb, kb) in enumerate(zip(q_blocks, k_blocks)):
        q_ref[:, h * LANES:(h + 1) * LANES] = (qb * scale).astype(BF16)
        k_ref[:, h * LANES:(h + 1) * LANES] = kb
    v_ref[...] = v
    cw = u.shape[1]
    w = wconv_ref[...]
    s0 = st_ref[:, 0:cw]
    s1 = st_ref[:, cw:2 * cw]
    c = s0 * w[0:1, :] + s1 * w[1:2, :] + u * w[2:3, :]
    gbc_ref[...] = (gb * c).astype(BF16)
    convs_ref[:, 0:cw] = s1
    convs_ref[:, cw:2 * cw] = u


def _inproj_decode(x, cos, sin, gat, gq, gk, win, gmat, wconv, state):
    n, d = x.shape
    aw = gmat.shape[0]
    cw = wconv.shape[1]
    nh = aw // LANES
    body = functools.partial(_inproj_decode_body, aw=aw, scale=(LANES // 2) ** -0.5)
    full = lambda shape: pl.BlockSpec(shape, lambda i: (0,) * len(shape))
    return pl.pallas_call(
        body,
        grid=(1,),
        in_specs=[full((n, d)), full((1, LANES)), full((1, LANES)), full((1, d)), full((1, aw)),
                  full((1, aw)), full(win.shape), full((aw, aw)), full((3, cw)), full((n, 2 * cw))],
        out_specs=[full((n, aw)), full((n, aw)), full((n, aw)), full((n, cw)), full((n, 2 * cw))],
        out_shape=[
            jax.ShapeDtypeStruct((n, aw), BF16),
            jax.ShapeDtypeStruct((n, aw), F32),
            jax.ShapeDtypeStruct((n, aw), F32),
            jax.ShapeDtypeStruct((n, cw), BF16),
            jax.ShapeDtypeStruct((n, 2 * cw), F32),
        ],
        compiler_params=pltpu.CompilerParams(
            dimension_semantics=("arbitrary",), vmem_limit_bytes=VMEM_LIMIT_BYTES),
        name="inproj_decode",
    )(x, cos, sin, gat, gq, gk, win, gmat, wconv, state)


def _subln(o, gsub, post_scale):
    ms = jnp.mean(o * o, axis=-1, keepdims=True)
    return o * lax.rsqrt(ms + EPS) * gsub * post_scale


def _attn_body(pt_ref, lq1_ref, lk1_ref, lq2_ref, lk2_ref, gcol_ref, grow_ref, exp_ref,
               qt_ref, k_ref, vt_ref, qd_ref, kn_ref, vn_ref, ck_hbm, cv_hbm,
               o_ref, od_ref,
               qz_scr, s_scr, mb_scr, m_scr, l_scr, acc_scr, kbuf, vbuf, ksem, vsem, dstate,
               *, tq, tk, n_pages, cp, spp, lam_init):
    assert tq == 2 * tk
    i = pl.program_id(2)
    step = (pl.program_id(0) * pl.num_programs(1) + pl.program_id(1)) * pl.num_programs(2) + i
    n_steps = pl.num_programs(0) * pl.num_programs(1) * pl.num_programs(2)
    nc = n_pages // cp
    per_step = spp * nc
    nh = od_ref.shape[0]
    lam = _lam_value(lq1_ref[...], lk1_ref[...], lq2_ref[...], lk2_ref[...], lam_init)

    n_slots = kbuf.shape[0]
    assert per_step % n_slots == 0

    def copies(at_step, pos, which):
        first_page = (at_step * spp + pos // nc) * n_pages + (pos % nc) * cp
        hbm, buf, sem = ((ck_hbm, kbuf, ksem), (cv_hbm, vbuf, vsem))[which]
        slot = pos % n_slots
        return [pltpu.make_async_copy(hbm.at[pt_ref[first_page + p]], buf.at[slot, p], sem.at[slot])
                for p in range(cp)]

    def start_all(at_step, jobs):
        for pos, which in jobs:
            for cpy in copies(at_step, pos, which):
                cpy.start()

    def decode_point(point):
        chunks = [pos for pos in range(per_step) if (pos * 4) // per_step == point]
        jobs = [(pos, which) for pos in chunks for which in (0, 1)]
        if not jobs:
            return
        for pos, which in jobs:
            for cpy in copies(step, pos, which):
                cpy.wait()
        state = None
        for pos in chunks:
            t, c, slot = pos // nc, pos % nc, pos % n_slots
            q = qd_ref[t]
            if c == 0:
                state = _decode_new_token(q, kn_ref[t], vn_ref[t])
            elif state is None:
                state = (dstate[0], dstate[1], dstate[2])
            state = _decode_chunk(q, kbuf.at[slot], vbuf.at[slot], exp_ref[...], state)
            if c == nc - 1:
                a = _decode_output(state, lam, grow_ref[...], lam_init)
                for h in range(nh):
                    od_ref[h, t] = a[2 * h:2 * h + 1, :].astype(BF16)
                state = None
        if state is not None:
            for n, v in enumerate(state):
                dstate[n] = v
        refill = [(pos + n_slots, which) for pos, which in jobs]
        start_all(step, [(pos, which) for pos, which in refill if pos < per_step])
        wrapped = [(pos - per_step, which) for pos, which in refill if pos >= per_step]
        if wrapped:
            @pl.when(step + 1 < n_steps)
            def _():
                start_all(step + 1, wrapped)

    @pl.when(step == 0)
    def _():
        start_all(0, [(pos, which) for pos in range(n_slots) for which in (0, 1)])

    decode_point(0)
    n2 = 2 * tq
    qt = qt_ref[...]
    sub = lax.broadcasted_iota(jnp.int32, qt.shape, 0)
    zero = jnp.zeros_like(qt)
    qz_scr[:, 0:tq] = jnp.where(sub < LANES // 2, qt, zero)
    qz_scr[:, tq:n2] = jnp.where(sub >= LANES // 2, qt, zero)
    m_scr[...] = jnp.full(m_scr.shape, NEG, F32)
    l_scr[...] = jnp.zeros(l_scr.shape, F32)
    acc_scr[...] = jnp.zeros(acc_scr.shape, F32)

    def stage_a(start, buf, diag_offset):
        kblk = k_ref[pl.ds(pl.multiple_of(start, tk), tk), :]
        s = jnp.dot(kblk, qz_scr[...], preferred_element_type=F32)
        if diag_offset is not None:
            key = lax.broadcasted_iota(jnp.int32, s.shape, 0) + diag_offset
            col = lax.broadcasted_iota(jnp.int32, s.shape, 1)
            qry = jnp.where(col >= tq, col - tq, col)
            s = jnp.where(key <= qry, s, NEG)
        s_scr[buf] = s
        mb_scr[buf] = jnp.max(s, axis=0, keepdims=True)

    def stage_b(start, buf):
        m_prev = m_scr[...]
        m_new = jnp.maximum(m_prev, mb_scr[buf])
        alpha = jnp.exp2(m_prev - m_new)
        p = jnp.exp2(s_scr[buf] - m_new)
        l_scr[...] = alpha * l_scr[...] + jnp.sum(p, axis=0, keepdims=True)
        vt = vt_ref[:, pl.ds(pl.multiple_of(start, tk), tk)]
        acc_scr[...] = alpha * acc_scr[...] + jnp.dot(vt, p.astype(BF16),
                                                      preferred_element_type=F32)
        m_scr[...] = m_new

    diag0 = i * tq
    stage_a(diag0, 0, 0)
    decode_point(1)

    def pair(u, carry):
        first = 2 * u * tk
        stage_a(first, 1, None)
        stage_b(jnp.where(u == 0, diag0, first - tk), 0)
        stage_a(first + tk, 0, None)
        stage_b(first, 1)
        return carry

    lax.fori_loop(0, i, pair, 0)
    decode_point(2)
    stage_a(diag0 + tk, 1, tk)
    stage_b(jnp.where(i == 0, diag0, diag0 - tk), 0)
    stage_b(diag0 + tk, 1)
    decode_point(3)

    ot = (acc_scr[:, 0:tq] / l_scr[:, 0:tq]
          - lam * (acc_scr[:, tq:n2] / l_scr[:, tq:n2]))
    ms = jnp.mean(ot * ot, axis=0, keepdims=True)
    at = ot * lax.rsqrt(ms + EPS) * gcol_ref[...] * (1.0 - lam_init)
    o_ref[...] = at.T.astype(BF16)


def _attention(pt_flat, lams, gsub, expand, qt, k, vt, qd, kn, vn, ck, cv,
               *, tq, tk, n_pages, cp, lam_init):
    b, nh, s, _ = k.shape
    n, _, aw = qd.shape
    hd = lams[0].shape[1]
    nq = s // tq
    n_steps = b * nh * nq
    assert n % n_steps == 0, "decode samples must split evenly over the attention grid steps"
    spp = n // n_steps
    per_step = spp * (n_pages // cp)
    n_slots = max(c for c in (2, 4, 8) if per_step % c == 0)
    body = functools.partial(_attn_body, tq=tq, tk=tk, n_pages=n_pages, cp=cp, spp=spp,
                             lam_init=lam_init)
    const = lambda shape: pl.BlockSpec(shape, lambda bi, h, i, pt: (0,) * len(shape))
    sample_rows = pl.BlockSpec((spp, 1, aw), lambda bi, h, i, pt: ((bi * nh + h) * nq + i, 0, 0))
    grid_spec = pltpu.PrefetchScalarGridSpec(
        num_scalar_prefetch=1,
        grid=(b, nh, nq),
        in_specs=[const((1, hd))] * 4 + [
            const((LANES, 1)), const((1, LANES)), const(expand.shape),
            pl.BlockSpec((None, None, LANES, tq), lambda bi, h, i, pt: (bi, h, 0, i)),
            pl.BlockSpec((None, None, s, LANES), lambda bi, h, i, pt: (bi, h, 0, 0)),
            pl.BlockSpec((None, None, LANES, s), lambda bi, h, i, pt: (bi, h, 0, 0)),
            sample_rows, sample_rows, sample_rows,
            pl.BlockSpec(memory_space=pl.ANY),
            pl.BlockSpec(memory_space=pl.ANY),
        ],
        out_specs=[
            pl.BlockSpec((None, None, tq, LANES), lambda bi, h, i, pt: (bi, h, i, 0)),
            pl.BlockSpec((nh, spp, 1, LANES), lambda bi, h, i, pt: (0, (bi * nh + h) * nq + i, 0, 0)),
        ],
        scratch_shapes=[
            pltpu.VMEM((LANES, 2 * tq), BF16),
            pltpu.VMEM((2, tk, 2 * tq), F32),
            pltpu.VMEM((2, 1, 2 * tq), F32),
            pltpu.VMEM((1, 2 * tq), F32),
            pltpu.VMEM((1, 2 * tq), F32),
            pltpu.VMEM((LANES, 2 * tq), F32),
            pltpu.VMEM((n_slots, cp) + ck.shape[1:], F32),
            pltpu.VMEM((n_slots, cp) + cv.shape[1:], F32),
            pltpu.SemaphoreType.DMA((n_slots,)),
            pltpu.SemaphoreType.DMA((n_slots,)),
            pltpu.VMEM((3, SUBLANES, LANES), F32),
        ],
    )
    return pl.pallas_call(
        body,
        grid_spec=grid_spec,
        out_shape=[jax.ShapeDtypeStruct((b, nh, s, LANES), BF16),
                   jax.ShapeDtypeStruct((nh, n, 1, LANES), BF16)],
        compiler_params=pltpu.CompilerParams(
            dimension_semantics=("arbitrary", "arbitrary", "arbitrary"),
            vmem_limit_bytes=VMEM_LIMIT_BYTES),
        name="attention",
    )(pt_flat, *lams, gsub.reshape(LANES, 1), gsub, expand, qt, k, vt, qd, kn, vn, ck, cv)


def _decode_query_rows(q):
    aw = q.shape[-1]
    row = lax.broadcasted_iota(jnp.int32, (SUBLANES, aw), 0)
    lane = lax.broadcasted_iota(jnp.int32, (SUBLANES, aw), 1)
    return jnp.where(lane // (LANES // 2) == row, q.astype(F32), 0.0)


def _decode_new_token(q, kn, vn):
    nh = q.shape[-1] // LANES
    m = jnp.sum(_decode_query_rows(q) * kn, axis=1, keepdims=True)
    m = jnp.broadcast_to(m, (SUBLANES, LANES))
    l = jnp.ones((SUBLANES, LANES), F32)
    row_v = lax.broadcasted_iota(jnp.int32, (SUBLANES, LANES), 0)
    acc = jnp.zeros((SUBLANES, LANES), F32)
    for h in range(nh):
        acc = jnp.where(row_v // 2 == h, vn[:, h * LANES:(h + 1) * LANES], acc)
    return m, l, acc


def _decode_chunk(q, kpages, vpages, expand, state):
    m, l, acc = state
    cp = kpages.shape[0]
    nh = q.shape[-1] // LANES
    qbd_b = _decode_query_rows(q).astype(BF16)
    s = [jnp.dot(qbd_b, kpages[p].astype(BF16), preferred_element_type=F32)
         for p in range(cp)]
    s_max = functools.reduce(jnp.maximum, s)
    m_new = jnp.maximum(m, jnp.max(s_max, axis=1, keepdims=True))
    alpha = jnp.exp(m - m_new)
    p_all = jnp.concatenate([jnp.exp(sp - m_new) for sp in s], axis=0)
    l = alpha * l + jnp.sum(
        jnp.sum(p_all.reshape(cp, SUBLANES, LANES), axis=0), axis=1, keepdims=True)
    w_all = jnp.dot(p_all.astype(BF16), expand, preferred_element_type=F32)
    row_e = lax.broadcasted_iota(jnp.int32, (SUBLANES, nh * PAGE_SIZE), 0)
    lane_e = lax.broadcasted_iota(jnp.int32, (SUBLANES, nh * PAGE_SIZE), 1)
    own_head = lane_e % nh == row_e // 2
    pv = jnp.zeros((SUBLANES, LANES), F32)
    for p in range(cp):
        w_p = jnp.where(own_head, w_all[p * SUBLANES:(p + 1) * SUBLANES, :], 0.0).astype(BF16)
        pv = pv + jnp.dot(w_p, vpages[p].astype(BF16), preferred_element_type=F32)
    return m_new, l, alpha * acc + pv


def _decode_output(state, lam, gsub_row, lam_init):
    _, l, acc = state
    row_v = lax.broadcasted_iota(jnp.int32, (SUBLANES, LANES), 0)
    t = (acc / l) * jnp.where(row_v % 2 == 0, 1.0, -lam)
    o = t + pltpu.roll(t, SUBLANES - 1, 0)
    return _subln(o, gsub_row, 1.0 - lam_init)


def _finish_body(x_ref, a_ref, gbc_ref, wout_ref, gmlp_ref, wup_ref, wdown_ref, y_ref, *, fc):
    nh = a_ref.shape[0]
    mix = jnp.concatenate([a_ref[h] for h in range(nh)] + [gbc_ref[...]], axis=1)
    h = x_ref[...] + jnp.dot(mix, wout_ref[...], preferred_element_type=F32)
    ms = jnp.mean(h * h, axis=-1, keepdims=True)
    hn = (h * lax.rsqrt(ms + EPS) * gmlp_ref[...]).astype(BF16)
    y = h
    dff = wup_ref.shape[1]
    for c in range(dff // fc):
        z = jnp.dot(hn, wup_ref[:, c * fc:(c + 1) * fc], preferred_element_type=F32)
        r = jnp.maximum(z, 0.0)
        y = y + jnp.dot((r * r).astype(BF16), wdown_ref[c * fc:(c + 1) * fc, :],
                        preferred_element_type=F32)
    y_ref[...] = y


def _finish(x, a, gbc, wout, gmlp, wup, wdown, *, tm):
    b, s, d = x.shape
    nh = a.shape[1]
    cw = gbc.shape[2]
    dff = wup.shape[1]
    body = functools.partial(_finish_body, fc=_pick(dff, 1024))
    return pl.pallas_call(
        body,
        grid=(b, s // tm),
        in_specs=[
            pl.BlockSpec((None, tm, d), lambda i, j: (i, j, 0)),
            pl.BlockSpec((None, nh, tm, LANES), lambda i, j: (i, 0, j, 0)),
            pl.BlockSpec((None, tm, cw), lambda i, j: (i, j, 0)),
            _const_spec((d, d), 2),
            _const_spec((1, d), 2),
            _const_spec((d, dff), 2),
            _const_spec((dff, d), 2),
        ],
        out_specs=pl.BlockSpec((None, tm, d), lambda i, j: (i, j, 0)),
        out_shape=jax.ShapeDtypeStruct((b, s, d), F32),
        compiler_params=pltpu.CompilerParams(
            dimension_semantics=("arbitrary", "arbitrary"),
            vmem_limit_bytes=VMEM_LIMIT_BYTES),
        name="finish",
    )(x, a, gbc, wout, gmlp, wup, wdown)


def _rope_angles(pos, half):
    inv = ROPE_THETA ** (-jnp.arange(half, dtype=F32) / half)
    return pos.astype(F32)[:, None] * inv[None, :]


def _rope_tables(pos, half):
    ang = _rope_angles(pos, half)
    cos = jnp.cos(ang)
    sin = jnp.sin(ang)
    return (jnp.concatenate([cos, cos, cos, cos], axis=1),
            jnp.concatenate([-sin, sin, -sin, sin], axis=1))


def kernel(x_prompt, x_sample, cache_k, cache_v, state_conv, page_table, g_attn_norm, w_in, g_q,
           g_k, lam_q1, lam_k1, lam_q2, lam_k2, g_subln, w_conv, w_out, g_mlp_norm, w_up, w_down):
    b, s, d = x_prompt.shape
    db, dec_seq, _ = x_sample.shape
    assert dec_seq == 1, "decode path handles one new token per sample"
    depth = w_in.shape[0]
    hd = g_q.shape[-1]
    vd = g_subln.shape[-1]
    assert vd == LANES and 2 * hd == LANES, "one head must fill one 128-lane block"
    aw = d // 2
    cw = d - aw
    nh = aw // vd
    n_pool = cache_k.shape[1]
    n_pages = page_table.shape[1]
    past = n_pages * PAGE_SIZE
    assert cache_k.shape[2] == PAGE_SIZE

    tm = _pick(s, 512)
    assert s % 2 == 0
    tk = _pick(s // 2, 512)
    tq = 2 * tk
    assert n_pages % 2 == 0
    cp = _pick(n_pages // 2, 16)

    ang_p = _rope_angles(jnp.arange(s), hd // 2).T
    cos_p, sin_p = jnp.cos(ang_p), jnp.sin(ang_p)
    cos_s, sin_s = _rope_tables(past + jnp.arange(1), hd // 2)
    grp = jnp.arange(aw) // hd
    gmat = ((grp[:, None] == grp[None, :]).astype(F32) / hd).astype(BF16)

    ck = jnp.transpose(cache_k, (0, 1, 3, 4, 5, 2)).reshape(depth * n_pool, aw, PAGE_SIZE)
    cv = cache_v.reshape(depth * n_pool, PAGE_SIZE * nh, vd)
    slot_of = jnp.arange(PAGE_SIZE * nh) // nh
    expand = (jnp.arange(PAGE_SIZE)[:, None] == slot_of[None, :]).astype(BF16)

    xp = x_prompt
    xs = x_sample.reshape(db, d)
    kp, vp, cpr, ksl, vsl, csl = [], [], [], [], [], []
    for l in range(depth):
        lam_init = 0.8 - 0.6 * math.exp(-0.3 * l)
        lams = tuple(v[l].reshape(1, hd).astype(F32) for v in (lam_q1, lam_k1, lam_q2, lam_k2))
        gat = g_attn_norm[l].reshape(1, d)
        gq = jnp.tile(g_q[l], aw // hd).reshape(1, aw)
        gk = jnp.tile(g_k[l], aw // hd).reshape(1, aw)
        gsub = g_subln[l].reshape(1, vd)
        gmlp = g_mlp_norm[l].reshape(1, d)
        win = w_in[l].astype(BF16)
        wout = w_out[l].astype(BF16)
        wup = w_up[l].astype(BF16)
        wdown = w_down[l].astype(BF16)
        wconv = w_conv[l]

        gq_rows = jnp.broadcast_to(jnp.tile(g_q[l], 2)[:, None], (LANES, LANES))
        gk_rows = jnp.broadcast_to(jnp.tile(g_k[l], 2)[:, None], (LANES, LANES))
        qt, kt, v, kb, vt, gbc, convp = _inproj_prompt(xp, cos_p, sin_p, gat, gq_rows, gk_rows,
                                                       win, wconv, tm=tm)
        qd, kn, vn, gbcd, convs = _inproj_decode(xs, cos_s, sin_s, gat, gq, gk, win, gmat, wconv,
                                                 state_conv[l].reshape(db, 2 * cw))
        pt_flat = (page_table + l * n_pool).reshape(-1).astype(jnp.int32)
        a, ad = _attention(pt_flat, lams, gsub, expand, qt, kb, vt, qd.reshape(db, 1, aw),
                           kn.reshape(db, 1, aw), vn.reshape(db, 1, aw), ck, cv,
                           tq=tq, tk=tk, n_pages=n_pages, cp=cp, lam_init=lam_init)
        xp = _finish(xp, a, gbc, wout, gmlp, wup, wdown, tm=tm)
        kp.append(jnp.transpose(kt.reshape(b, nh, 2, hd, s), (0, 4, 1, 2, 3)))
        vp.append(v.reshape(b, s, nh, vd))
        cpr.append(convp)
        xs = _finish(xs.reshape(1, db, d), ad.reshape(1, nh, db, LANES), gbcd.reshape(1, db, cw),
                     wout, gmlp, wup, wdown, tm=db).reshape(db, d)
        ksl.append(kn.reshape(db, 1, nh, 2, hd))
        vsl.append(vn.reshape(db, 1, nh, vd))
        csl.append(convs.reshape(db, 2, cw))

    return (xp, xs.reshape(db, 1, d), jnp.stack(kp), jnp.stack(vp), jnp.stack(cpr),
            jnp.stack(ksl), jnp.stack(vsl), jnp.stack(csl))
```

```python
import functools
import math

import jax
import jax.numpy as jnp
from jax import lax
from jax.experimental import pallas as pl
from jax.experimental.pallas import tpu as pltpu

F32 = jnp.float32
BF16 = jnp.bfloat16

PAGE_SIZE = 128
ROPE_THETA = 10000.0
EPS = 1e-6
NEG = -1e30

LANES = 128
SUBLANES = 8
VMEM_LIMIT_BYTES = 60 * 1024 * 1024


def _pick(n, target):
    t = min(n, target)
    while n % t:
        t -= 1
    return t


def _const_spec(shape, n_grid):
    zeros = (0,) * len(shape)
    if n_grid == 1:
        return pl.BlockSpec(shape, lambda a: zeros, pipeline_mode=pl.Buffered(1))
    if n_grid == 2:
        return pl.BlockSpec(shape, lambda a, b: zeros, pipeline_mode=pl.Buffered(1))
    return pl.BlockSpec(shape, lambda a, b, c: zeros, pipeline_mode=pl.Buffered(1))


def _lam_value(lq1, lk1, lq2, lk2, lam_init):
    s1 = jnp.sum(lq1 * lk1, axis=-1, keepdims=True)
    s2 = jnp.sum(lq2 * lk2, axis=-1, keepdims=True)
    return jnp.exp(s1) - jnp.exp(s2) + lam_init


def _group_rms_rope(z, gain, gmat, cos, sin):
    zz = z * z
    hi = zz.astype(BF16)
    lo = (zz - hi.astype(F32)).astype(BF16)
    g = gmat
    ms = (jnp.dot(hi, g, preferred_element_type=F32)
          + jnp.dot(lo, g, preferred_element_type=F32))
    zn = z * lax.rsqrt(ms + EPS) * gain
    t = z.shape[0]
    lane = lax.broadcasted_iota(jnp.int32, (t, LANES), 1)
    upper = (lane & 32) != 0
    outs = []
    for c in range(z.shape[1] // LANES):
        blk = zn[:, c * LANES:(c + 1) * LANES]
        swapped = jnp.where(upper, pltpu.roll(blk, 32, 1), pltpu.roll(blk, LANES - 32, 1))
        outs.append(blk * cos + swapped * sin)
    return outs


def _project(x, gat, win, gq, gk, gmat, cos, sin, aw):
    ms = jnp.mean(x * x, axis=-1, keepdims=True)
    n = (x * lax.rsqrt(ms + EPS) * gat).astype(BF16)
    z = jnp.dot(n, win, preferred_element_type=F32)
    q_blocks = _group_rms_rope(z[:, 0:aw], gq, gmat, cos, sin)
    k_blocks = _group_rms_rope(z[:, aw:2 * aw], gk, gmat, cos, sin)
    v = z[:, 2 * aw:3 * aw]
    cw = (z.shape[1] - 3 * aw) // 3
    gb = z[:, 3 * aw:3 * aw + cw]
    gc = z[:, 3 * aw + cw:3 * aw + 2 * cw]
    h = z[:, 3 * aw + 2 * cw:]
    return q_blocks, k_blocks, v, gb, gc * h


def _head_norm_rope_t(zt, gain, cos, sin):
    hd = zt.shape[0] // 2
    rot = hd // 2
    t = zt.shape[1]
    halves = []
    for c in range(2):
        x = zt[c * hd:(c + 1) * hd, :]
        ms = jnp.mean(x * x, axis=0, keepdims=True)
        halves.append(x * lax.rsqrt(ms + EPS))
    zn = jnp.concatenate(halves, axis=0)
    zn = jnp.concatenate([zn[:, j * LANES:(j + 1) * LANES] * gain for j in range(t // LANES)],
                         axis=1)
    out = []
    for c in range(2):
        x1 = zn[c * hd:c * hd + rot, :]
        x2 = zn[c * hd + rot:(c + 1) * hd, :]
        out += [x1 * cos - x2 * sin, x2 * cos + x1 * sin]
    return jnp.concatenate(out, axis=0)


def _inproj_prompt_body(x_ref, cos_ref, sin_ref, gat_ref, gq_ref, gk_ref, win_ref,
                        wconv_ref, qt_ref, kt_ref, v_ref, kb_ref, vt_ref, gbc_ref, convp_ref,
                        ubuf, *, tm, aw, scale):
    j = pl.program_id(1)
    nh = aw // LANES
    x = x_ref[...]
    ms = jnp.mean(x * x, axis=-1, keepdims=True)
    n = (x * lax.rsqrt(ms + EPS) * gat_ref[...]).astype(BF16)
    z = jnp.dot(n, win_ref[...], preferred_element_type=F32)
    cw = (z.shape[1] - 3 * aw) // 3
    gb = z[:, 3 * aw:3 * aw + cw]
    u = z[:, 3 * aw + cw:3 * aw + 2 * cw] * z[:, 3 * aw + 2 * cw:]
    cos = cos_ref[...]
    sin = sin_ref[...]
    for h in range(nh):
        cols = slice(h * LANES, (h + 1) * LANES)
        qt = _head_norm_rope_t(z[:, cols].T, gq_ref[...], cos, sin)
        kt = _head_norm_rope_t(z[:, aw:2 * aw][:, cols].T, gk_ref[...], cos, sin)
        vh = z[:, 2 * aw:3 * aw][:, cols]
        qt_ref[h] = (qt * scale).astype(BF16)
        vt_ref[h] = vh.T.astype(BF16)
        kt_ref[cols, :] = kt
        kb_ref[h] = kt.T.astype(BF16)
        v_ref[pl.ds(h, tm, stride=nh), :] = vh

    @pl.when(j == 0)
    def _():
        ubuf[0:SUBLANES, :] = jnp.zeros((SUBLANES, ubuf.shape[1]), F32)

    ubuf[SUBLANES:SUBLANES + tm, :] = u
    w = wconv_ref[...]
    c = (ubuf[SUBLANES - 2:SUBLANES - 2 + tm, :] * w[0:1, :]
         + ubuf[SUBLANES - 1:SUBLANES - 1 + tm, :] * w[1:2, :]
         + u * w[2:3, :])
    gbc_ref[...] = (gb * c).astype(BF16)
    convp_ref[...] = ubuf[SUBLANES + tm - 2:SUBLANES + tm, :]
    ubuf[0:SUBLANES, :] = ubuf[tm:tm + SUBLANES, :]


def _inproj_prompt(x, cos, sin, gat, gq, gk, win, wconv, *, tm):
    b, s, d = x.shape
    cw = wconv.shape[1]
    pw = win.shape[1]
    aw = (pw - 3 * cw) // 3
    nh = aw // LANES
    rot = cos.shape[0]
    grid = (b, s // tm)
    body = functools.partial(_inproj_prompt_body, tm=tm, aw=aw,
                             scale=(LANES // 2) ** -0.5 * math.log2(math.e))
    return pl.pallas_call(
        body,
        grid=grid,
        in_specs=[
            pl.BlockSpec((None, tm, d), lambda i, j: (i, j, 0)),
            pl.BlockSpec((rot, tm), lambda i, j: (0, j)),
            pl.BlockSpec((rot, tm), lambda i, j: (0, j)),
            _const_spec((1, d), 2),
            _const_spec((LANES, LANES), 2),
            _const_spec((LANES, LANES), 2),
            _const_spec((d, pw), 2),
            _const_spec((3, cw), 2),
        ],
        out_specs=[
            pl.BlockSpec((None, nh, LANES, tm), lambda i, j: (i, 0, 0, j)),
            pl.BlockSpec((None, aw, tm), lambda i, j: (i, 0, j)),
            pl.BlockSpec((None, tm * nh, LANES), lambda i, j: (i, j, 0)),
            pl.BlockSpec((None, nh, tm, LANES), lambda i, j: (i, 0, j, 0)),
            pl.BlockSpec((None, nh, LANES, tm), lambda i, j: (i, 0, 0, j)),
            pl.BlockSpec((None, tm, cw), lambda i, j: (i, j, 0)),
            pl.BlockSpec((None, 2, cw), lambda i, j: (i, 0, 0)),
        ],
        out_shape=[
            jax.ShapeDtypeStruct((b, nh, LANES, s), BF16),
            jax.ShapeDtypeStruct((b, aw, s), F32),
            jax.ShapeDtypeStruct((b, s * nh, LANES), F32),
            jax.ShapeDtypeStruct((b, nh, s, LANES), BF16),
            jax.ShapeDtypeStruct((b, nh, LANES, s), BF16),
            jax.ShapeDtypeStruct((b, s, cw), BF16),
            jax.ShapeDtypeStruct((b, 2, cw), F32),
        ],
        scratch_shapes=[pltpu.VMEM((tm + SUBLANES, cw), F32)],
        compiler_params=pltpu.CompilerParams(
            dimension_semantics=("arbitrary", "arbitrary"),
            vmem_limit_bytes=VMEM_LIMIT_BYTES),
        name="inproj_prompt",
    )(x, cos, sin, gat, gq, gk, win, wconv)


def _inproj_decode_body(x_ref, cos_ref, sin_ref, gat_ref, gq_ref, gk_ref, win_ref, gmat_ref,
                        wconv_ref, st_ref, q_ref, k_ref, v_ref, gbc_ref, convs_ref, *, aw, scale):
    q_blocks, k_blocks, v, gb, u = _project(
        x_ref[...], gat_ref[...], win_ref[...], gq_ref[...], gk_ref[...], gmat_ref[...],
        cos_ref[...], sin_ref[...], aw)
    for h, (qb, kb) in enumerate(zip(q_blocks, k_blocks)):
        q_ref[:, h * LANES:(h + 1) * LANES] = (qb * scale).astype(BF16)
        k_ref[:, h * LANES:(h + 1) * LANES] = kb
    v_ref[...] = v
    cw = u.shape[1]
    w = wconv_ref[...]
    s0 = st_ref[:, 0:cw]
    s1 = st_ref[:, cw:2 * cw]
    c = s0 * w[0:1, :] + s1 * w[1:2, :] + u * w[2:3, :]
    gbc_ref[...] = (gb * c).astype(BF16)
    convs_ref[:, 0:cw] = s1
    convs_ref[:, cw:2 * cw] = u


def _inproj_decode(x, cos, sin, gat, gq, gk, win, gmat, wconv, state):
    n, d = x.shape
    aw = gmat.shape[0]
    cw = wconv.shape[1]
    nh = aw // LANES
    body = functools.partial(_inproj_decode_body, aw=aw, scale=(LANES // 2) ** -0.5)
    full = lambda shape: pl.BlockSpec(shape, lambda i: (0,) * len(shape))
    return pl.pallas_call(
        body,
        grid=(1,),
        in_specs=[full((n, d)), full((1, LANES)), full((1, LANES)), full((1, d)), full((1, aw)),
                  full((1, aw)), full(win.shape), full((aw, aw)), full((3, cw)), full((n, 2 * cw))],
        out_specs=[full((n, aw)), full((n, aw)), full((n, aw)), full((n, cw)), full((n, 2 * cw))],
        out_shape=[
            jax.ShapeDtypeStruct((n, aw), BF16),
            jax.ShapeDtypeStruct((n, aw), F32),
            jax.ShapeDtypeStruct((n, aw), F32),
            jax.ShapeDtypeStruct((n, cw), BF16),
            jax.ShapeDtypeStruct((n, 2 * cw), F32),
        ],
        compiler_params=pltpu.CompilerParams(
            dimension_semantics=("arbitrary",), vmem_limit_bytes=VMEM_LIMIT_BYTES),
        name="inproj_decode",
    )(x, cos, sin, gat, gq, gk, win, gmat, wconv, state)


def _subln(o, gsub, post_scale):
    ms = jnp.mean(o * o, axis=-1, keepdims=True)
    return o * lax.rsqrt(ms + EPS) * gsub * post_scale


def _attn_body(pt_ref, lq1_ref, lk1_ref, lq2_ref, lk2_ref, gcol_ref, grow_ref, exp_ref,
               qt_ref, k_ref, vt_ref, qd_ref, kn_ref, vn_ref, ck_hbm, cv_hbm,
               o_ref, od_ref,
               qz_scr, s_scr, mb_scr, m_scr, l_scr, acc_scr, kbuf, vbuf, ksem, vsem, dstate,
               sdec, smax, *, tq, tk, n_pages, cp, spp, lam_init):
    assert tq == 2 * tk
    i = pl.program_id(2)
    step = (pl.program_id(0) * pl.num_programs(1) + pl.program_id(1)) * pl.num_programs(2) + i
    n_steps = pl.num_programs(0) * pl.num_programs(1) * pl.num_programs(2)
    nc = n_pages // cp
    per_step = spp * nc
    nh = od_ref.shape[0]
    lam = _lam_value(lq1_ref[...], lk1_ref[...], lq2_ref[...], lk2_ref[...], lam_init)

    n_slots = kbuf.shape[0]
    assert per_step % n_slots == 0

    def copies(at_step, pos, which):
        first_page = (at_step * spp + pos // nc) * n_pages + (pos % nc) * cp
        hbm, buf, sem = ((ck_hbm, kbuf, ksem), (cv_hbm, vbuf, vsem))[which]
        slot = pos % n_slots
        return [pltpu.make_async_copy(hbm.at[pt_ref[first_page + p]], buf.at[slot, p], sem.at[slot])
                for p in range(cp)]

    def start_all(at_step, jobs):
        for pos, which in jobs:
            for cpy in copies(at_step, pos, which):
                cpy.start()

    def decode_point(point):
        values = [pos for pos in range(per_step) if (pos * 4) // per_step == point]
        scores = [pos + 1 for pos in values if pos + 1 < per_step]
        if point == 0:
            scores = [0] + scores
        jobs = [(pos, 0) for pos in scores] + [(pos, 1) for pos in values]
        if not jobs:
            return
        for pos, which in jobs:
            for cpy in copies(step, pos, which):
                cpy.wait()
        for pos in scores:
            _decode_scores(qd_ref[pos // nc], kbuf.at[pos % n_slots], sdec.at[pos % 2],
                           smax.at[pos % 2])
        state = None
        for pos in values:
            t, c = pos // nc, pos % nc
            if c == 0:
                state = _decode_new_token(qd_ref[t], kn_ref[t], vn_ref[t])
            elif state is None:
                state = (dstate[0], dstate[1], dstate[2])
            state = _decode_values(sdec.at[pos % 2], smax.at[pos % 2], vbuf.at[pos % n_slots],
                                   exp_ref[...], state)
            if c == nc - 1:
                a = _decode_output(state, lam, grow_ref[...], lam_init)
                for h in range(nh):
                    od_ref[h, t] = a[2 * h:2 * h + 1, :].astype(BF16)
                state = None
        if state is not None:
            for n, v in enumerate(state):
                dstate[n] = v
        refill = [(pos + n_slots, which) for pos, which in jobs]
        start_all(step, [(pos, which) for pos, which in refill if pos < per_step])
        wrapped = [(pos - per_step, which) for pos, which in refill if pos >= per_step]
        if wrapped:
            @pl.when(step + 1 < n_steps)
            def _():
                start_all(step + 1, wrapped)

    @pl.when(step == 0)
    def _():
        start_all(0, [(pos, which) for pos in range(n_slots) for which in (0, 1)])

    decode_point(0)
    n2 = 2 * tq
    qt = qt_ref[...]
    sub = lax.broadcasted_iota(jnp.int32, qt.shape, 0)
    zero = jnp.zeros_like(qt)
    qz_scr[:, 0:tq] = jnp.where(sub < LANES // 2, qt, zero)
    qz_scr[:, tq:n2] = jnp.where(sub >= LANES // 2, qt, zero)
    m_scr[...] = jnp.full(m_scr.shape, NEG, F32)
    l_scr[...] = jnp.zeros(l_scr.shape, F32)
    acc_scr[...] = jnp.zeros(acc_scr.shape, F32)

    all_cols = ((0, n2),)
    late_cols = ((tq // 2, tq), (tq + tq // 2, n2))

    def stage_a(start, buf, diag_offset, cols=all_cols):
        kblk = k_ref[pl.ds(pl.multiple_of(start, tk), tk), :]
        for lo, hi in cols:
            s = jnp.dot(kblk, qz_scr[:, lo:hi], preferred_element_type=F32)
            if diag_offset is not None:
                key = lax.broadcasted_iota(jnp.int32, s.shape, 0) + diag_offset
                col = lax.broadcasted_iota(jnp.int32, s.shape, 1) + lo
                qry = jnp.where(col >= tq, col - tq, col)
                s = jnp.where(key <= qry, s, NEG)
            s_scr[buf, :, lo:hi] = s
            mb_scr[buf, :, lo:hi] = jnp.max(s, axis=0, keepdims=True)

    def stage_b(start, buf, cols=all_cols):
        vt = vt_ref[:, pl.ds(pl.multiple_of(start, tk), tk)]
        for lo, hi in cols:
            m_prev = m_scr[:, lo:hi]
            m_new = jnp.maximum(m_prev, mb_scr[buf, :, lo:hi])
            alpha = jnp.exp2(m_prev - m_new)
            p = jnp.exp2(s_scr[buf, :, lo:hi] - m_new)
            l_scr[:, lo:hi] = alpha * l_scr[:, lo:hi] + jnp.sum(p, axis=0, keepdims=True)
            acc_scr[:, lo:hi] = alpha * acc_scr[:, lo:hi] + jnp.dot(vt, p.astype(BF16),
                                                                    preferred_element_type=F32)
            m_scr[:, lo:hi] = m_new

    diag0 = i * tq
    stage_a(diag0, 0, 0)
    decode_point(1)

    def pair(u, carry):
        first = 2 * u * tk
        stage_a(first, 1, None)
        stage_b(jnp.where(u == 0, diag0, first - tk), 0)
        stage_a(first + tk, 0, None)
        stage_b(first, 1)
        return carry

    lax.fori_loop(0, i, pair, 0)
    decode_point(2)
    stage_a(diag0 + tk, 1, tk, late_cols)
    stage_b(jnp.where(i == 0, diag0, diag0 - tk), 0)
    stage_b(diag0 + tk, 1, late_cols)
    decode_point(3)

    ot = (acc_scr[:, 0:tq] / l_scr[:, 0:tq]
          - lam * (acc_scr[:, tq:n2] / l_scr[:, tq:n2]))
    ms = jnp.mean(ot * ot, axis=0, keepdims=True)
    at = ot * lax.rsqrt(ms + EPS) * gcol_ref[...] * (1.0 - lam_init)
    o_ref[...] = at.T.astype(BF16)


def _attention(pt_flat, lams, gsub, expand, qt, k, vt, qd, kn, vn, ck, cv,
               *, tq, tk, n_pages, cp, lam_init):
    b, nh, s, _ = k.shape
    n, _, aw = qd.shape
    hd = lams[0].shape[1]
    nq = s // tq
    n_steps = b * nh * nq
    assert n % n_steps == 0, "decode samples must split evenly over the attention grid steps"
    spp = n // n_steps
    per_step = spp * (n_pages // cp)
    n_slots = max(c for c in (2, 4, 8) if per_step % c == 0)
    body = functools.partial(_attn_body, tq=tq, tk=tk, n_pages=n_pages, cp=cp, spp=spp,
                             lam_init=lam_init)
    const = lambda shape: pl.BlockSpec(shape, lambda bi, h, i, pt: (0,) * len(shape))
    sample_rows = pl.BlockSpec((spp, 1, aw), lambda bi, h, i, pt: ((bi * nh + h) * nq + i, 0, 0))
    grid_spec = pltpu.PrefetchScalarGridSpec(
        num_scalar_prefetch=1,
        grid=(b, nh, nq),
        in_specs=[const((1, hd))] * 4 + [
            const((LANES, 1)), const((1, LANES)), const(expand.shape),
            pl.BlockSpec((None, None, LANES, tq), lambda bi, h, i, pt: (bi, h, 0, i)),
            pl.BlockSpec((None, None, s, LANES), lambda bi, h, i, pt: (bi, h, 0, 0)),
            pl.BlockSpec((None, None, LANES, s), lambda bi, h, i, pt: (bi, h, 0, 0)),
            sample_rows, sample_rows, sample_rows,
            pl.BlockSpec(memory_space=pl.ANY),
            pl.BlockSpec(memory_space=pl.ANY),
        ],
        out_specs=[
            pl.BlockSpec((None, None, tq, LANES), lambda bi, h, i, pt: (bi, h, i, 0)),
            pl.BlockSpec((nh, spp, 1, LANES), lambda bi, h, i, pt: (0, (bi * nh + h) * nq + i, 0, 0)),
        ],
        scratch_shapes=[
            pltpu.VMEM((LANES, 2 * tq), BF16),
            pltpu.VMEM((2, tk, 2 * tq), F32),
            pltpu.VMEM((2, 1, 2 * tq), F32),
            pltpu.VMEM((1, 2 * tq), F32),
            pltpu.VMEM((1, 2 * tq), F32),
            pltpu.VMEM((LANES, 2 * tq), F32),
            pltpu.VMEM((n_slots, cp) + ck.shape[1:], F32),
            pltpu.VMEM((n_slots, cp) + cv.shape[1:], F32),
            pltpu.SemaphoreType.DMA((n_slots,)),
            pltpu.SemaphoreType.DMA((n_slots,)),
            pltpu.VMEM((3, SUBLANES, LANES), F32),
            pltpu.VMEM((2, cp * SUBLANES, LANES), F32),
            pltpu.VMEM((2, SUBLANES, LANES), F32),
        ],
    )
    return pl.pallas_call(
        body,
        grid_spec=grid_spec,
        out_shape=[jax.ShapeDtypeStruct((b, nh, s, LANES), BF16),
                   jax.ShapeDtypeStruct((nh, n, 1, LANES), BF16)],
        compiler_params=pltpu.CompilerParams(
            dimension_semantics=("arbitrary", "arbitrary", "arbitrary"),
            vmem_limit_bytes=VMEM_LIMIT_BYTES),
        name="attention",
    )(pt_flat, *lams, gsub.reshape(LANES, 1), gsub, expand, qt, k, vt, qd, kn, vn, ck, cv)


def _decode_query_rows(q):
    aw = q.shape[-1]
    row = lax.broadcasted_iota(jnp.int32, (SUBLANES, aw), 0)
    lane = lax.broadcasted_iota(jnp.int32, (SUBLANES, aw), 1)
    return jnp.where(lane // (LANES // 2) == row, q.astype(F32), 0.0)


def _decode_new_token(q, kn, vn):
    nh = q.shape[-1] // LANES
    m = jnp.sum(_decode_query_rows(q) * kn, axis=1, keepdims=True)
    m = jnp.broadcast_to(m, (SUBLANES, LANES))
    l = jnp.ones((SUBLANES, LANES), F32)
    row_v = lax.broadcasted_iota(jnp.int32, (SUBLANES, LANES), 0)
    acc = jnp.zeros((SUBLANES, LANES), F32)
    for h in range(nh):
        acc = jnp.where(row_v // 2 == h, vn[:, h * LANES:(h + 1) * LANES], acc)
    return m, l, acc


def _decode_scores(q, kpages, s_out, smax_out):
    cp = kpages.shape[0]
    qbd_b = _decode_query_rows(q).astype(BF16)
    s = [jnp.dot(qbd_b, kpages[p].astype(BF16), preferred_element_type=F32)
         for p in range(cp)]
    for p in range(cp):
        s_out[p * SUBLANES:(p + 1) * SUBLANES, :] = s[p]
    smax_out[...] = functools.reduce(jnp.maximum, s)


def _decode_values(s_in, smax_in, vpages, expand, state):
    m, l, acc = state
    cp = vpages.shape[0]
    nh = vpages.shape[1] // PAGE_SIZE
    m_new = jnp.maximum(m, jnp.max(smax_in[...], axis=1, keepdims=True))
    alpha = jnp.exp(m - m_new)
    p3 = jnp.exp(s_in[...].reshape(cp, SUBLANES, LANES) - m_new[None])
    l = alpha * l + jnp.sum(jnp.sum(p3, axis=0), axis=1, keepdims=True)
    p_all = p3.reshape(cp * SUBLANES, LANES)
    w_all = jnp.dot(p_all.astype(BF16), expand, preferred_element_type=F32)
    row_e = lax.broadcasted_iota(jnp.int32, (SUBLANES, nh * PAGE_SIZE), 0)
    lane_e = lax.broadcasted_iota(jnp.int32, (SUBLANES, nh * PAGE_SIZE), 1)
    own_head = lane_e % nh == row_e // 2
    pv = jnp.zeros((SUBLANES, LANES), F32)
    for p in range(cp):
        w_p = jnp.where(own_head, w_all[p * SUBLANES:(p + 1) * SUBLANES, :], 0.0).astype(BF16)
        pv = pv + jnp.dot(w_p, vpages[p].astype(BF16), preferred_element_type=F32)
    return m_new, l, alpha * acc + pv


def _decode_output(state, lam, gsub_row, lam_init):
    _, l, acc = state
    row_v = lax.broadcasted_iota(jnp.int32, (SUBLANES, LANES), 0)
    t = (acc / l) * jnp.where(row_v % 2 == 0, 1.0, -lam)
    o = t + pltpu.roll(t, SUBLANES - 1, 0)
    return _subln(o, gsub_row, 1.0 - lam_init)


def _finish_body(x_ref, a_ref, gbc_ref, wout_ref, gmlp_ref, wup_ref, wdown_ref, y_ref, *, fc):
    nh = a_ref.shape[0]
    mix = jnp.concatenate([a_ref[h] for h in range(nh)] + [gbc_ref[...]], axis=1)
    h = x_ref[...] + jnp.dot(mix, wout_ref[...], preferred_element_type=F32)
    ms = jnp.mean(h * h, axis=-1, keepdims=True)
    hn = (h * lax.rsqrt(ms + EPS) * gmlp_ref[...]).astype(BF16)
    y = h
    dff = wup_ref.shape[1]
    for c in range(dff // fc):
        z = jnp.dot(hn, wup_ref[:, c * fc:(c + 1) * fc], preferred_element_type=F32)
        r = jnp.maximum(z, 0.0)
        y = y + jnp.dot((r * r).astype(BF16), wdown_ref[c * fc:(c + 1) * fc, :],
                        preferred_element_type=F32)
    y_ref[...] = y


def _finish(x, a, gbc, wout, gmlp, wup, wdown, *, tm):
    b, s, d = x.shape
    nh = a.shape[1]
    cw = gbc.shape[2]
    dff = wup.shape[1]
    body = functools.partial(_finish_body, fc=_pick(dff, 1024))
    return pl.pallas_call(
        body,
        grid=(b, s // tm),
        in_specs=[
            pl.BlockSpec((None, tm, d), lambda i, j: (i, j, 0)),
            pl.BlockSpec((None, nh, tm, LANES), lambda i, j: (i, 0, j, 0)),
            pl.BlockSpec((None, tm, cw), lambda i, j: (i, j, 0)),
            _const_spec((d, d), 2),
            _const_spec((1, d), 2),
            _const_spec((d, dff), 2),
            _const_spec((dff, d), 2),
        ],
        out_specs=pl.BlockSpec((None, tm, d), lambda i, j: (i, j, 0)),
        out_shape=jax.ShapeDtypeStruct((b, s, d), F32),
        compiler_params=pltpu.CompilerParams(
            dimension_semantics=("arbitrary", "arbitrary"),
            vmem_limit_bytes=VMEM_LIMIT_BYTES),
        name="finish",
    )(x, a, gbc, wout, gmlp, wup, wdown)


def _rope_angles(pos, half):
    inv = ROPE_THETA ** (-jnp.arange(half, dtype=F32) / half)
    return pos.astype(F32)[:, None] * inv[None, :]


def _rope_tables(pos, half):
    ang = _rope_angles(pos, half)
    cos = jnp.cos(ang)
    sin = jnp.sin(ang)
    return (jnp.concatenate([cos, cos, cos, cos], axis=1),
            jnp.concatenate([-sin, sin, -sin, sin], axis=1))


def kernel(x_prompt, x_sample, cache_k, cache_v, state_conv, page_table, g_attn_norm, w_in, g_q,
           g_k, lam_q1, lam_k1, lam_q2, lam_k2, g_subln, w_conv, w_out, g_mlp_norm, w_up, w_down):
    b, s, d = x_prompt.shape
    db, dec_seq, _ = x_sample.shape
    assert dec_seq == 1, "decode path handles one new token per sample"
    depth = w_in.shape[0]
    hd = g_q.shape[-1]
    vd = g_subln.shape[-1]
    assert vd == LANES and 2 * hd == LANES, "one head must fill one 128-lane block"
    aw = d // 2
    cw = d - aw
    nh = aw // vd
    n_pool = cache_k.shape[1]
    n_pages = page_table.shape[1]
    past = n_pages * PAGE_SIZE
    assert cache_k.shape[2] == PAGE_SIZE

    tm = _pick(s, 512)
    assert s % 2 == 0
    tk = _pick(s // 2, 512)
    tq = 2 * tk
    assert n_pages % 2 == 0
    cp = _pick(n_pages // 2, 16)

    ang_p = _rope_angles(jnp.arange(s), hd // 2).T
    cos_p, sin_p = jnp.cos(ang_p), jnp.sin(ang_p)
    cos_s, sin_s = _rope_tables(past + jnp.arange(1), hd // 2)
    grp = jnp.arange(aw) // hd
    gmat = ((grp[:, None] == grp[None, :]).astype(F32) / hd).astype(BF16)

    ck = jnp.transpose(cache_k, (0, 1, 3, 4, 5, 2)).reshape(depth * n_pool, aw, PAGE_SIZE)
    cv = cache_v.reshape(depth * n_pool, PAGE_SIZE * nh, vd)
    slot_of = jnp.arange(PAGE_SIZE * nh) // nh
    expand = (jnp.arange(PAGE_SIZE)[:, None] == slot_of[None, :]).astype(BF16)

    xp = x_prompt
    xs = x_sample.reshape(db, d)
    kp, vp, cpr, ksl, vsl, csl = [], [], [], [], [], []
    for l in range(depth):
        lam_init = 0.8 - 0.6 * math.exp(-0.3 * l)
        lams = tuple(v[l].reshape(1, hd).astype(F32) for v in (lam_q1, lam_k1, lam_q2, lam_k2))
        gat = g_attn_norm[l].reshape(1, d)
        gq = jnp.tile(g_q[l], aw // hd).reshape(1, aw)
        gk = jnp.tile(g_k[l], aw // hd).reshape(1, aw)
        gsub = g_subln[l].reshape(1, vd)
        gmlp = g_mlp_norm[l].reshape(1, d)
        win = w_in[l].astype(BF16)
        wout = w_out[l].astype(BF16)
        wup = w_up[l].astype(BF16)
        wdown = w_down[l].astype(BF16)
        wconv = w_conv[l]

        gq_rows = jnp.broadcast_to(jnp.tile(g_q[l], 2)[:, None], (LANES, LANES))
        gk_rows = jnp.broadcast_to(jnp.tile(g_k[l], 2)[:, None], (LANES, LANES))
        qt, kt, v, kb, vt, gbc, convp = _inproj_prompt(xp, cos_p, sin_p, gat, gq_rows, gk_rows,
                                                       win, wconv, tm=tm)
        qd, kn, vn, gbcd, convs = _inproj_decode(xs, cos_s, sin_s, gat, gq, gk, win, gmat, wconv,
                                                 state_conv[l].reshape(db, 2 * cw))
        pt_flat = (page_table + l * n_pool).reshape(-1).astype(jnp.int32)
        a, ad = _attention(pt_flat, lams, gsub, expand, qt, kb, vt, qd.reshape(db, 1, aw),
                           kn.reshape(db, 1, aw), vn.reshape(db, 1, aw), ck, cv,
                           tq=tq, tk=tk, n_pages=n_pages, cp=cp, lam_init=lam_init)
        xp = _finish(xp, a, gbc, wout, gmlp, wup, wdown, tm=tm)
        kp.append(jnp.transpose(kt.reshape(b, nh, 2, hd, s), (0, 4, 1, 2, 3)))
        vp.append(v.reshape(b, s, nh, vd))
        cpr.append(convp)
        xs = _finish(xs.reshape(1, db, d), ad.reshape(1, nh, db, LANES), gbcd.reshape(1, db, cw),
                     wout, gmlp, wup, wdown, tm=db).reshape(db, d)
        ksl.append(kn.reshape(db, 1, nh, 2, hd))
        vsl.append(vn.reshape(db, 1, nh, vd))
        csl.append(convs.reshape(db, 2, cw))

    return (xp, xs.reshape(db, 1, d), jnp.stack(kp), jnp.stack(vp), jnp.stack(cpr),
            jnp.stack(ksl), jnp.stack(vsl), jnp.stack(csl))
```

```python
import functools
import math

import jax
import jax.numpy as jnp
from jax import lax
from jax.experimental import pallas as pl
from jax.experimental.pallas import tpu as pltpu

F32 = jnp.float32
BF16 = jnp.bfloat16

PAGE_SIZE = 128
ROPE_THETA = 10000.0
EPS = 1e-6
NEG = -1e30

LANES = 128
SUBLANES = 8
VMEM_LIMIT_BYTES = 60 * 1024 * 1024
ATTN_COL_BLOCK = 512


def _pick(n, target):
    t = min(n, target)
    while n % t:
        t -= 1
    return t


def _const_spec(shape, n_grid):
    zeros = (0,) * len(shape)
    if n_grid == 1:
        return pl.BlockSpec(shape, lambda a: zeros, pipeline_mode=pl.Buffered(1))
    if n_grid == 2:
        return pl.BlockSpec(shape, lambda a, b: zeros, pipeline_mode=pl.Buffered(1))
    return pl.BlockSpec(shape, lambda a, b, c: zeros, pipeline_mode=pl.Buffered(1))


def _lam_value(lq1, lk1, lq2, lk2, lam_init):
    s1 = jnp.sum(lq1 * lk1, axis=-1, keepdims=True)
    s2 = jnp.sum(lq2 * lk2, axis=-1, keepdims=True)
    return jnp.exp(s1) - jnp.exp(s2) + lam_init


def _group_rms_rope(z, gain, gmat, cos, sin):
    zz = z * z
    hi = zz.astype(BF16)
    lo = (zz - hi.astype(F32)).astype(BF16)
    g = gmat
    ms = (jnp.dot(hi, g, preferred_element_type=F32)
          + jnp.dot(lo, g, preferred_element_type=F32))
    zn = z * lax.rsqrt(ms + EPS) * gain
    t = z.shape[0]
    lane = lax.broadcasted_iota(jnp.int32, (t, LANES), 1)
    upper = (lane & 32) != 0
    outs = []
    for c in range(z.shape[1] // LANES):
        blk = zn[:, c * LANES:(c + 1) * LANES]
        swapped = jnp.where(upper, pltpu.roll(blk, 32, 1), pltpu.roll(blk, LANES - 32, 1))
        outs.append(blk * cos + swapped * sin)
    return outs


def _project(x, gat, win, gq, gk, gmat, cos, sin, aw):
    ms = jnp.mean(x * x, axis=-1, keepdims=True)
    n = (x * lax.rsqrt(ms + EPS) * gat).astype(BF16)
    z = jnp.dot(n, win, preferred_element_type=F32)
    q_blocks = _group_rms_rope(z[:, 0:aw], gq, gmat, cos, sin)
    k_blocks = _group_rms_rope(z[:, aw:2 * aw], gk, gmat, cos, sin)
    v = z[:, 2 * aw:3 * aw]
    cw = (z.shape[1] - 3 * aw) // 3
    gb = z[:, 3 * aw:3 * aw + cw]
    gc = z[:, 3 * aw + cw:3 * aw + 2 * cw]
    h = z[:, 3 * aw + 2 * cw:]
    return q_blocks, k_blocks, v, gb, gc * h


def _head_norm_rope_t(zt, gain, cos, sin):
    hd = zt.shape[0] // 2
    rot = hd // 2
    t = zt.shape[1]
    halves = []
    for c in range(2):
        x = zt[c * hd:(c + 1) * hd, :]
        ms = jnp.mean(x * x, axis=0, keepdims=True)
        halves.append(x * lax.rsqrt(ms + EPS))
    zn = jnp.concatenate(halves, axis=0)
    zn = jnp.concatenate([zn[:, j * LANES:(j + 1) * LANES] * gain for j in range(t // LANES)],
                         axis=1)
    out = []
    for c in range(2):
        x1 = zn[c * hd:c * hd + rot, :]
        x2 = zn[c * hd + rot:(c + 1) * hd, :]
        out += [x1 * cos - x2 * sin, x2 * cos + x1 * sin]
    return jnp.concatenate(out, axis=0)


def _inproj_prompt_body(x_ref, cos_ref, sin_ref, gat_ref, gq_ref, gk_ref, win_ref,
                        wconv_ref, qt_ref, kt_ref, v_ref, kb_ref, vt_ref, gbc_ref, convp_ref,
                        ubuf, *, tm, aw, scale):
    j = pl.program_id(1)
    nh = aw // LANES
    x = x_ref[...]
    ms = jnp.mean(x * x, axis=-1, keepdims=True)
    n = (x * lax.rsqrt(ms + EPS) * gat_ref[...]).astype(BF16)
    cw = (win_ref.shape[1] - 3 * aw) // 3

    def project(lo, hi):
        return jnp.dot(n, win_ref[:, lo:hi], preferred_element_type=F32)

    cos = cos_ref[...]
    sin = sin_ref[...]
    zq = project(0, aw)
    for h in range(nh):
        qt = _head_norm_rope_t(zq[:, h * LANES:(h + 1) * LANES].T, gq_ref[...], cos, sin)
        qt_ref[h] = (qt * scale).astype(BF16)
    zk = project(aw, 2 * aw)
    for h in range(nh):
        cols = slice(h * LANES, (h + 1) * LANES)
        kt = _head_norm_rope_t(zk[:, cols].T, gk_ref[...], cos, sin)
        kt_ref[cols, :] = kt
        kb_ref[h] = kt.T.astype(BF16)
    zv = project(2 * aw, 3 * aw)
    for h in range(nh):
        vh = zv[:, h * LANES:(h + 1) * LANES]
        vt_ref[h] = vh.T.astype(BF16)
        v_ref[pl.ds(h, tm, stride=nh), :] = vh
    gb = project(3 * aw, 3 * aw + cw)
    u = project(3 * aw + cw, 3 * aw + 2 * cw) * project(3 * aw + 2 * cw, 3 * aw + 3 * cw)

    @pl.when(j == 0)
    def _():
        ubuf[0:SUBLANES, :] = jnp.zeros((SUBLANES, ubuf.shape[1]), F32)

    ubuf[SUBLANES:SUBLANES + tm, :] = u
    w = wconv_ref[...]
    c = (ubuf[SUBLANES - 2:SUBLANES - 2 + tm, :] * w[0:1, :]
         + ubuf[SUBLANES - 1:SUBLANES - 1 + tm, :] * w[1:2, :]
         + u * w[2:3, :])
    gbc_ref[...] = (gb * c).astype(BF16)
    convp_ref[...] = ubuf[SUBLANES + tm - 2:SUBLANES + tm, :]
    ubuf[0:SUBLANES, :] = ubuf[tm:tm + SUBLANES, :]


def _inproj_prompt(x, cos, sin, gat, gq, gk, win, wconv, *, tm):
    b, s, d = x.shape
    cw = wconv.shape[1]
    pw = win.shape[1]
    aw = (pw - 3 * cw) // 3
    nh = aw // LANES
    rot = cos.shape[0]
    grid = (b, s // tm)
    body = functools.partial(_inproj_prompt_body, tm=tm, aw=aw,
                             scale=(LANES // 2) ** -0.5 * math.log2(math.e))
    return pl.pallas_call(
        body,
        grid=grid,
        in_specs=[
            pl.BlockSpec((None, tm, d), lambda i, j: (i, j, 0)),
            pl.BlockSpec((rot, tm), lambda i, j: (0, j)),
            pl.BlockSpec((rot, tm), lambda i, j: (0, j)),
            _const_spec((1, d), 2),
            _const_spec((LANES, LANES), 2),
            _const_spec((LANES, LANES), 2),
            _const_spec((d, pw), 2),
            _const_spec((3, cw), 2),
        ],
        out_specs=[
            pl.BlockSpec((None, nh, LANES, tm), lambda i, j: (i, 0, 0, j)),
            pl.BlockSpec((None, aw, tm), lambda i, j: (i, 0, j)),
            pl.BlockSpec((None, tm * nh, LANES), lambda i, j: (i, j, 0)),
            pl.BlockSpec((None, nh, tm, LANES), lambda i, j: (i, 0, j, 0)),
            pl.BlockSpec((None, nh, LANES, tm), lambda i, j: (i, 0, 0, j)),
            pl.BlockSpec((None, tm, cw), lambda i, j: (i, j, 0)),
            pl.BlockSpec((None, 2, cw), lambda i, j: (i, 0, 0)),
        ],
        out_shape=[
            jax.ShapeDtypeStruct((b, nh, LANES, s), BF16),
            jax.ShapeDtypeStruct((b, aw, s), F32),
            jax.ShapeDtypeStruct((b, s * nh, LANES), F32),
            jax.ShapeDtypeStruct((b, nh, s, LANES), BF16),
            jax.ShapeDtypeStruct((b, nh, LANES, s), BF16),
            jax.ShapeDtypeStruct((b, s, cw), BF16),
            jax.ShapeDtypeStruct((b, 2, cw), F32),
        ],
        scratch_shapes=[pltpu.VMEM((tm + SUBLANES, cw), F32)],
        compiler_params=pltpu.CompilerParams(
            dimension_semantics=("arbitrary", "arbitrary"),
            vmem_limit_bytes=VMEM_LIMIT_BYTES),
        name="inproj_prompt",
    )(x, cos, sin, gat, gq, gk, win, wconv)


def _inproj_decode_body(x_ref, cos_ref, sin_ref, gat_ref, gq_ref, gk_ref, win_ref, gmat_ref,
                        wconv_ref, st_ref, q_ref, k_ref, v_ref, gbc_ref, convs_ref, *, aw, scale):
    q_blocks, k_blocks, v, gb, u = _project(
        x_ref[...], gat_ref[...], win_ref[...], gq_ref[...], gk_ref[...], gmat_ref[...],
        cos_ref[...], sin_ref[...], aw)
    for h, (qb, kb) in enumerate(zip(q_blocks, k_blocks)):
        q_ref[:, h * LANES:(h + 1) * LANES] = (qb * scale).astype(BF16)
        k_ref[:, h * LANES:(h + 1) * LANES] = kb
    v_ref[...] = v
    cw = u.shape[1]
    w = wconv_ref[...]
    s0 = st_ref[:, 0:cw]
    s1 = st_ref[:, cw:2 * cw]
    c = s0 * w[0:1, :] + s1 * w[1:2, :] + u * w[2:3, :]
    gbc_ref[...] = (gb * c).astype(BF16)
    convs_ref[:, 0:cw] = s1
    convs_ref[:, cw:2 * cw] = u


def _inproj_decode(x, cos, sin, gat, gq, gk, win, gmat, wconv, state):
    n, d = x.shape
    aw = gmat.shape[0]
    cw = wconv.shape[1]
    nh = aw // LANES
    body = functools.partial(_inproj_decode_body, aw=aw, scale=(LANES // 2) ** -0.5)
    full = lambda shape: pl.BlockSpec(shape, lambda i: (0,) * len(shape))
    return pl.pallas_call(
        body,
        grid=(1,),
        in_specs=[full((n, d)), full((1, LANES)), full((1, LANES)), full((1, d)), full((1, aw)),
                  full((1, aw)), full(win.shape), full((aw, aw)), full((3, cw)), full((n, 2 * cw))],
        out_specs=[full((n, aw)), full((n, aw)), full((n, aw)), full((n, cw)), full((n, 2 * cw))],
        out_shape=[
            jax.ShapeDtypeStruct((n, aw), BF16),
            jax.ShapeDtypeStruct((n, aw), F32),
            jax.ShapeDtypeStruct((n, aw), F32),
            jax.ShapeDtypeStruct((n, cw), BF16),
            jax.ShapeDtypeStruct((n, 2 * cw), F32),
        ],
        compiler_params=pltpu.CompilerParams(
            dimension_semantics=("arbitrary",), vmem_limit_bytes=VMEM_LIMIT_BYTES),
        name="inproj_decode",
    )(x, cos, sin, gat, gq, gk, win, gmat, wconv, state)


def _subln(o, gsub, post_scale):
    ms = jnp.mean(o * o, axis=-1, keepdims=True)
    return o * lax.rsqrt(ms + EPS) * gsub * post_scale


def _attn_body(pt_ref, lq1_ref, lk1_ref, lq2_ref, lk2_ref, gcol_ref, grow_ref, exp_ref,
               qt_ref, k_ref, vt_ref, qd_ref, kn_ref, vn_ref, ck_hbm, cv_hbm,
               o_ref, od_ref,
               qz_scr, s_scr, mb_scr, m_scr, l_scr, acc_scr, kbuf, vbuf, ksem, vsem, dstate,
               sdec, smax, *, tq, tk, n_pages, cp, spp, lam_init):
    assert tq == 2 * tk
    i = pl.program_id(2)
    step = (pl.program_id(0) * pl.num_programs(1) + pl.program_id(1)) * pl.num_programs(2) + i
    n_steps = pl.num_programs(0) * pl.num_programs(1) * pl.num_programs(2)
    nc = n_pages // cp
    per_step = spp * nc
    nh = od_ref.shape[0]
    lam = _lam_value(lq1_ref[...], lk1_ref[...], lq2_ref[...], lk2_ref[...], lam_init)

    n_slots = kbuf.shape[0]
    assert per_step % n_slots == 0

    def copies(at_step, pos, which):
        first_page = (at_step * spp + pos // nc) * n_pages + (pos % nc) * cp
        hbm, buf, sem = ((ck_hbm, kbuf, ksem), (cv_hbm, vbuf, vsem))[which]
        slot = pos % n_slots
        return [pltpu.make_async_copy(hbm.at[pt_ref[first_page + p]], buf.at[slot, p], sem.at[slot])
                for p in range(cp)]

    def start_all(at_step, jobs):
        for pos, which in jobs:
            for cpy in copies(at_step, pos, which):
                cpy.start()

    def decode_point(point):
        values = [pos for pos in range(per_step) if (pos * 4) // per_step == point]
        scores = [pos + 1 for pos in values if pos + 1 < per_step]
        if point == 0:
            scores = [0] + scores
        jobs = [(pos, 0) for pos in scores] + [(pos, 1) for pos in values]
        if not jobs:
            return
        for pos, which in jobs:
            for cpy in copies(step, pos, which):
                cpy.wait()
        for pos in scores:
            _decode_scores(qd_ref[pos // nc], kbuf.at[pos % n_slots], sdec.at[pos % 2],
                           smax.at[pos % 2])
        state = None
        for pos in values:
            t, c = pos // nc, pos % nc
            if c == 0:
                state = _decode_new_token(qd_ref[t], kn_ref[t], vn_ref[t])
            elif state is None:
                state = (dstate[0], dstate[1], dstate[2])
            state = _decode_values(sdec.at[pos % 2], smax.at[pos % 2], vbuf.at[pos % n_slots],
                                   exp_ref[...], state)
            if c == nc - 1:
                a = _decode_output(state, lam, grow_ref[...], lam_init)
                for h in range(nh):
                    od_ref[h, t] = a[2 * h:2 * h + 1, :].astype(BF16)
                state = None
        if state is not None:
            for n, v in enumerate(state):
                dstate[n] = v
        refill = [(pos + n_slots, which) for pos, which in jobs]
        start_all(step, [(pos, which) for pos, which in refill if pos < per_step])
        wrapped = [(pos - per_step, which) for pos, which in refill if pos >= per_step]
        if wrapped:
            @pl.when(step + 1 < n_steps)
            def _():
                start_all(step + 1, wrapped)

    @pl.when(step == 0)
    def _():
        start_all(0, [(pos, which) for pos in range(n_slots) for which in (0, 1)])

    decode_point(0)
    n2 = 2 * tq
    qt = qt_ref[...]
    sub = lax.broadcasted_iota(jnp.int32, qt.shape, 0)
    zero = jnp.zeros_like(qt)
    qz_scr[:, 0:tq] = jnp.where(sub < LANES // 2, qt, zero)
    qz_scr[:, tq:n2] = jnp.where(sub >= LANES // 2, qt, zero)
    m_scr[...] = jnp.full(m_scr.shape, NEG, F32)
    l_scr[...] = jnp.zeros(l_scr.shape, F32)
    acc_scr[...] = jnp.zeros(acc_scr.shape, F32)

    all_cols = tuple((c, c + ATTN_COL_BLOCK) for c in range(0, n2, ATTN_COL_BLOCK))
    late_cols = ((tq // 2, tq), (tq + tq // 2, n2))

    def stage_a(start, buf, diag_offset, cols=all_cols):
        kblk = k_ref[pl.ds(pl.multiple_of(start, tk), tk), :]
        for lo, hi in cols:
            s = jnp.dot(kblk, qz_scr[:, lo:hi], preferred_element_type=F32)
            if diag_offset is not None:
                key = lax.broadcasted_iota(jnp.int32, s.shape, 0) + diag_offset
                col = lax.broadcasted_iota(jnp.int32, s.shape, 1) + lo
                qry = jnp.where(col >= tq, col - tq, col)
                s = jnp.where(key <= qry, s, NEG)
            s_scr[buf, :, lo:hi] = s
            mb_scr[buf, :, lo:hi] = jnp.max(s, axis=0, keepdims=True)

    def stage_b(start, buf, cols=all_cols):
        vt = vt_ref[:, pl.ds(pl.multiple_of(start, tk), tk)]
        for lo, hi in cols:
            m_prev = m_scr[:, lo:hi]
            m_new = jnp.maximum(m_prev, mb_scr[buf, :, lo:hi])
            alpha = jnp.exp2(m_prev - m_new)
            p = jnp.exp2(s_scr[buf, :, lo:hi] - m_new)
            l_scr[:, lo:hi] = alpha * l_scr[:, lo:hi] + jnp.sum(p, axis=0, keepdims=True)
            acc_scr[:, lo:hi] = alpha * acc_scr[:, lo:hi] + jnp.dot(vt, p.astype(BF16),
                                                                    preferred_element_type=F32)
            m_scr[:, lo:hi] = m_new

    diag0 = i * tq
    stage_a(diag0, 0, 0)
    decode_point(1)

    def pair(u, carry):
        first = 2 * u * tk
        before = jnp.where(u == 0, diag0, first - tk)
        for c in all_cols:
            stage_a(first, 1, None, (c,))
            stage_b(before, 0, (c,))
        for c in all_cols:
            stage_a(first + tk, 0, None, (c,))
            stage_b(first, 1, (c,))
        return carry

    lax.fori_loop(0, i, pair, 0)
    decode_point(2)
    before = jnp.where(i == 0, diag0, diag0 - tk)
    for n, c in enumerate(all_cols):
        if n < len(late_cols):
            stage_a(diag0 + tk, 1, tk, (late_cols[n],))
        stage_b(before, 0, (c,))
    stage_b(diag0 + tk, 1, late_cols)
    decode_point(3)

    ot = (acc_scr[:, 0:tq] / l_scr[:, 0:tq]
          - lam * (acc_scr[:, tq:n2] / l_scr[:, tq:n2]))
    ms = jnp.mean(ot * ot, axis=0, keepdims=True)
    at = ot * lax.rsqrt(ms + EPS) * gcol_ref[...] * (1.0 - lam_init)
    o_ref[...] = at.T.astype(BF16)


def _attention(pt_flat, lams, gsub, expand, qt, k, vt, qd, kn, vn, ck, cv,
               *, tq, tk, n_pages, cp, lam_init):
    b, nh, s, _ = k.shape
    n, _, aw = qd.shape
    hd = lams[0].shape[1]
    nq = s // tq
    n_steps = b * nh * nq
    assert n % n_steps == 0, "decode samples must split evenly over the attention grid steps"
    spp = n // n_steps
    per_step = spp * (n_pages // cp)
    n_slots = max(c for c in (2, 4, 8) if per_step % c == 0)
    body = functools.partial(_attn_body, tq=tq, tk=tk, n_pages=n_pages, cp=cp, spp=spp,
                             lam_init=lam_init)
    const = lambda shape: pl.BlockSpec(shape, lambda bi, h, i, pt: (0,) * len(shape))
    sample_rows = pl.BlockSpec((spp, 1, aw), lambda bi, h, i, pt: ((bi * nh + h) * nq + i, 0, 0))
    grid_spec = pltpu.PrefetchScalarGridSpec(
        num_scalar_prefetch=1,
        grid=(b, nh, nq),
        in_specs=[const((1, hd))] * 4 + [
            const((LANES, 1)), const((1, LANES)), const(expand.shape),
            pl.BlockSpec((None, None, LANES, tq), lambda bi, h, i, pt: (bi, h, 0, i)),
            pl.BlockSpec((None, None, s, LANES), lambda bi, h, i, pt: (bi, h, 0, 0)),
            pl.BlockSpec((None, None, LANES, s), lambda bi, h, i, pt: (bi, h, 0, 0)),
            sample_rows, sample_rows, sample_rows,
            pl.BlockSpec(memory_space=pl.ANY),
            pl.BlockSpec(memory_space=pl.ANY),
        ],
        out_specs=[
            pl.BlockSpec((None, None, tq, LANES), lambda bi, h, i, pt: (bi, h, i, 0)),
            pl.BlockSpec((nh, spp, 1, LANES), lambda bi, h, i, pt: (0, (bi * nh + h) * nq + i, 0, 0)),
        ],
        scratch_shapes=[
            pltpu.VMEM((LANES, 2 * tq), BF16),
            pltpu.VMEM((2, tk, 2 * tq), F32),
            pltpu.VMEM((2, 1, 2 * tq), F32),
            pltpu.VMEM((1, 2 * tq), F32),
            pltpu.VMEM((1, 2 * tq), F32),
            pltpu.VMEM((LANES, 2 * tq), F32),
            pltpu.VMEM((n_slots, cp) + ck.shape[1:], F32),
            pltpu.VMEM((n_slots, cp) + cv.shape[1:], F32),
            pltpu.SemaphoreType.DMA((n_slots,)),
            pltpu.SemaphoreType.DMA((n_slots,)),
            pltpu.VMEM((3, SUBLANES, LANES), F32),
            pltpu.VMEM((2, cp * SUBLANES, LANES), F32),
            pltpu.VMEM((2, SUBLANES, LANES), F32),
        ],
    )
    return pl.pallas_call(
        body,
        grid_spec=grid_spec,
        out_shape=[jax.ShapeDtypeStruct((b, nh, s, LANES), BF16),
                   jax.ShapeDtypeStruct((nh, n, 1, LANES), BF16)],
        compiler_params=pltpu.CompilerParams(
            dimension_semantics=("arbitrary", "arbitrary", "arbitrary"),
            vmem_limit_bytes=VMEM_LIMIT_BYTES),
        name="attention",
    )(pt_flat, *lams, gsub.reshape(LANES, 1), gsub, expand, qt, k, vt, qd, kn, vn, ck, cv)


def _decode_query_rows(q):
    aw = q.shape[-1]
    row = lax.broadcasted_iota(jnp.int32, (SUBLANES, aw), 0)
    lane = lax.broadcasted_iota(jnp.int32, (SUBLANES, aw), 1)
    return jnp.where(lane // (LANES // 2) == row, q.astype(F32), 0.0)


def _decode_new_token(q, kn, vn):
    nh = q.shape[-1] // LANES
    m = jnp.sum(_decode_query_rows(q) * kn, axis=1, keepdims=True)
    m = jnp.broadcast_to(m, (SUBLANES, LANES))
    l = jnp.ones((SUBLANES, LANES), F32)
    row_v = lax.broadcasted_iota(jnp.int32, (SUBLANES, LANES), 0)
    acc = jnp.zeros((SUBLANES, LANES), F32)
    for h in range(nh):
        acc = jnp.where(row_v // 2 == h, vn[:, h * LANES:(h + 1) * LANES], acc)
    return m, l, acc


def _decode_scores(q, kpages, s_out, smax_out):
    cp = kpages.shape[0]
    qbd_b = _decode_query_rows(q).astype(BF16)
    s = [jnp.dot(qbd_b, kpages[p].astype(BF16), preferred_element_type=F32)
         for p in range(cp)]
    for p in range(cp):
        s_out[p * SUBLANES:(p + 1) * SUBLANES, :] = s[p]
    smax_out[...] = functools.reduce(jnp.maximum, s)


def _decode_values(s_in, smax_in, vpages, expand, state):
    m, l, acc = state
    cp = vpages.shape[0]
    nh = vpages.shape[1] // PAGE_SIZE
    m_new = jnp.maximum(m, jnp.max(smax_in[...], axis=1, keepdims=True))
    alpha = jnp.exp(m - m_new)
    p3 = jnp.exp(s_in[...].reshape(cp, SUBLANES, LANES) - m_new[None])
    l = alpha * l + jnp.sum(jnp.sum(p3, axis=0), axis=1, keepdims=True)
    p_all = p3.reshape(cp * SUBLANES, LANES)
    w_all = jnp.dot(p_all.astype(BF16), expand, preferred_element_type=F32)
    row_e = lax.broadcasted_iota(jnp.int32, (SUBLANES, nh * PAGE_SIZE), 0)
    lane_e = lax.broadcasted_iota(jnp.int32, (SUBLANES, nh * PAGE_SIZE), 1)
    own_head = lane_e % nh == row_e // 2
    pv = jnp.zeros((SUBLANES, LANES), F32)
    for p in range(cp):
        w_p = jnp.where(own_head, w_all[p * SUBLANES:(p + 1) * SUBLANES, :], 0.0).astype(BF16)
        pv = pv + jnp.dot(w_p, vpages[p].astype(BF16), preferred_element_type=F32)
    return m_new, l, alpha * acc + pv


def _decode_output(state, lam, gsub_row, lam_init):
    _, l, acc = state
    row_v = lax.broadcasted_iota(jnp.int32, (SUBLANES, LANES), 0)
    t = (acc / l) * jnp.where(row_v % 2 == 0, 1.0, -lam)
    o = t + pltpu.roll(t, SUBLANES - 1, 0)
    return _subln(o, gsub_row, 1.0 - lam_init)


def _finish_body(x_ref, a_ref, gbc_ref, wout_ref, gmlp_ref, wup_ref, wdown_ref, y_ref, *, fc):
    nh = a_ref.shape[0]
    mix = jnp.concatenate([a_ref[h] for h in range(nh)] + [gbc_ref[...]], axis=1)
    h = x_ref[...] + jnp.dot(mix, wout_ref[...], preferred_element_type=F32)
    ms = jnp.mean(h * h, axis=-1, keepdims=True)
    hn = (h * lax.rsqrt(ms + EPS) * gmlp_ref[...]).astype(BF16)
    y = h
    dff = wup_ref.shape[1]
    for c in range(dff // fc):
        z = jnp.dot(hn, wup_ref[:, c * fc:(c + 1) * fc], preferred_element_type=F32)
        r = jnp.maximum(z, 0.0)
        y = y + jnp.dot((r * r).astype(BF16), wdown_ref[c * fc:(c + 1) * fc, :],
                        preferred_element_type=F32)
    y_ref[...] = y


def _finish(x, a, gbc, wout, gmlp, wup, wdown, *, tm):
    b, s, d = x.shape
    nh = a.shape[1]
    cw = gbc.shape[2]
    dff = wup.shape[1]
    body = functools.partial(_finish_body, fc=_pick(dff, 1024))
    return pl.pallas_call(
        body,
        grid=(b, s // tm),
        in_specs=[
            pl.BlockSpec((None, tm, d), lambda i, j: (i, j, 0)),
            pl.BlockSpec((None, nh, tm, LANES), lambda i, j: (i, 0, j, 0)),
            pl.BlockSpec((None, tm, cw), lambda i, j: (i, j, 0)),
            _const_spec((d, d), 2),
            _const_spec((1, d), 2),
            _const_spec((d, dff), 2),
            _const_spec((dff, d), 2),
        ],
        out_specs=pl.BlockSpec((None, tm, d), lambda i, j: (i, j, 0)),
        out_shape=jax.ShapeDtypeStruct((b, s, d), F32),
        compiler_params=pltpu.CompilerParams(
            dimension_semantics=("arbitrary", "arbitrary"),
            vmem_limit_bytes=VMEM_LIMIT_BYTES),
        name="finish",
    )(x, a, gbc, wout, gmlp, wup, wdown)


def _rope_angles(pos, half):
    inv = ROPE_THETA ** (-jnp.arange(half, dtype=F32) / half)
    return pos.astype(F32)[:, None] * inv[None, :]


def _rope_tables(pos, half):
    ang = _rope_angles(pos, half)
    cos = jnp.cos(ang)
    sin = jnp.sin(ang)
    return (jnp.concatenate([cos, cos, cos, cos], axis=1),
            jnp.concatenate([-sin, sin, -sin, sin], axis=1))


def kernel(x_prompt, x_sample, cache_k, cache_v, state_conv, page_table, g_attn_norm, w_in, g_q,
           g_k, lam_q1, lam_k1, lam_q2, lam_k2, g_subln, w_conv, w_out, g_mlp_norm, w_up, w_down):
    b, s, d = x_prompt.shape
    db, dec_seq, _ = x_sample.shape
    assert dec_seq == 1, "decode path handles one new token per sample"
    depth = w_in.shape[0]
    hd = g_q.shape[-1]
    vd = g_subln.shape[-1]
    assert vd == LANES and 2 * hd == LANES, "one head must fill one 128-lane block"
    aw = d // 2
    cw = d - aw
    nh = aw // vd
    n_pool = cache_k.shape[1]
    n_pages = page_table.shape[1]
    past = n_pages * PAGE_SIZE
    assert cache_k.shape[2] == PAGE_SIZE

    tm = _pick(s, 512)
    assert s % 2 == 0
    tk = _pick(s // 2, 512)
    tq = 2 * tk
    assert n_pages % 2 == 0
    cp = _pick(n_pages // 2, 16)

    ang_p = _rope_angles(jnp.arange(s), hd // 2).T
    cos_p, sin_p = jnp.cos(ang_p), jnp.sin(ang_p)
    cos_s, sin_s = _rope_tables(past + jnp.arange(1), hd // 2)
    grp = jnp.arange(aw) // hd
    gmat = ((grp[:, None] == grp[None, :]).astype(F32) / hd).astype(BF16)

    ck = jnp.transpose(cache_k, (0, 1, 3, 4, 5, 2)).reshape(depth * n_pool, aw, PAGE_SIZE)
    cv = cache_v.reshape(depth * n_pool, PAGE_SIZE * nh, vd)
    slot_of = jnp.arange(PAGE_SIZE * nh) // nh
    expand = (jnp.arange(PAGE_SIZE)[:, None] == slot_of[None, :]).astype(BF16)

    xp = x_prompt
    xs = x_sample.reshape(db, d)
    kp, vp, cpr, ksl, vsl, csl = [], [], [], [], [], []
    for l in range(depth):
        lam_init = 0.8 - 0.6 * math.exp(-0.3 * l)
        lams = tuple(v[l].reshape(1, hd).astype(F32) for v in (lam_q1, lam_k1, lam_q2, lam_k2))
        gat = g_attn_norm[l].reshape(1, d)
        gq = jnp.tile(g_q[l], aw // hd).reshape(1, aw)
        gk = jnp.tile(g_k[l], aw // hd).reshape(1, aw)
        gsub = g_subln[l].reshape(1, vd)
        gmlp = g_mlp_norm[l].reshape(1, d)
        win = w_in[l].astype(BF16)
        wout = w_out[l].astype(BF16)
        wup = w_up[l].astype(BF16)
        wdown = w_down[l].astype(BF16)
        wconv = w_conv[l]

        gq_rows = jnp.broadcast_to(jnp.tile(g_q[l], 2)[:, None], (LANES, LANES))
        gk_rows = jnp.broadcast_to(jnp.tile(g_k[l], 2)[:, None], (LANES, LANES))
        qt, kt, v, kb, vt, gbc, convp = _inproj_prompt(xp, cos_p, sin_p, gat, gq_rows, gk_rows,
                                                       win, wconv, tm=tm)
        qd, kn, vn, gbcd, convs = _inproj_decode(xs, cos_s, sin_s, gat, gq, gk, win, gmat, wconv,
                                                 state_conv[l].reshape(db, 2 * cw))
        pt_flat = (page_table + l * n_pool).reshape(-1).astype(jnp.int32)
        a, ad = _attention(pt_flat, lams, gsub, expand, qt, kb, vt, qd.reshape(db, 1, aw),
                           kn.reshape(db, 1, aw), vn.reshape(db, 1, aw), ck, cv,
                           tq=tq, tk=tk, n_pages=n_pages, cp=cp, lam_init=lam_init)
        xp = _finish(xp, a, gbc, wout, gmlp, wup, wdown, tm=tm)
        kp.append(jnp.transpose(kt.reshape(b, nh, 2, hd, s), (0, 4, 1, 2, 3)))
        vp.append(v.reshape(b, s, nh, vd))
        cpr.append(convp)
        xs = _finish(xs.reshape(1, db, d), ad.reshape(1, nh, db, LANES), gbcd.reshape(1, db, cw),
                     wout, gmlp, wup, wdown, tm=db).reshape(db, d)
        ksl.append(kn.reshape(db, 1, nh, 2, hd))
        vsl.append(vn.reshape(db, 1, nh, vd))
        csl.append(convs.reshape(db, 2, cw))

    return (xp, xs.reshape(db, 1, d), jnp.stack(kp), jnp.stack(vp), jnp.stack(cpr),
            jnp.stack(ksl), jnp.stack(vsl), jnp.stack(csl))
```

```python
import functools
import math

import jax
import jax.numpy as jnp
from jax import lax
from jax.experimental import pallas as pl
from jax.experimental.pallas import tpu as pltpu

F32 = jnp.float32
BF16 = jnp.bfloat16

PAGE_SIZE = 128
ROPE_THETA = 10000.0
EPS = 1e-6
NEG = -1e30

LANES = 128
SUBLANES = 8
VMEM_LIMIT_BYTES = 60 * 1024 * 1024
ATTN_COL_BLOCK = 512


def _pick(n, target):
    t = min(n, target)
    while n % t:
        t -= 1
    return t


def _const_spec(shape, n_grid):
    zeros = (0,) * len(shape)
    if n_grid == 1:
        return pl.BlockSpec(shape, lambda a: zeros, pipeline_mode=pl.Buffered(1))
    if n_grid == 2:
        return pl.BlockSpec(shape, lambda a, b: zeros, pipeline_mode=pl.Buffered(1))
    return pl.BlockSpec(shape, lambda a, b, c: zeros, pipeline_mode=pl.Buffered(1))


def _lam_value(lq1, lk1, lq2, lk2, lam_init):
    s1 = jnp.sum(lq1 * lk1, axis=-1, keepdims=True)
    s2 = jnp.sum(lq2 * lk2, axis=-1, keepdims=True)
    return jnp.exp(s1) - jnp.exp(s2) + lam_init


def _group_rms_rope(z, gain, gmat, cos, sin):
    zz = z * z
    hi = zz.astype(BF16)
    lo = (zz - hi.astype(F32)).astype(BF16)
    g = gmat
    ms = (jnp.dot(hi, g, preferred_element_type=F32)
          + jnp.dot(lo, g, preferred_element_type=F32))
    zn = z * lax.rsqrt(ms + EPS) * gain
    t = z.shape[0]
    lane = lax.broadcasted_iota(jnp.int32, (t, LANES), 1)
    upper = (lane & 32) != 0
    outs = []
    for c in range(z.shape[1] // LANES):
        blk = zn[:, c * LANES:(c + 1) * LANES]
        swapped = jnp.where(upper, pltpu.roll(blk, 32, 1), pltpu.roll(blk, LANES - 32, 1))
        outs.append(blk * cos + swapped * sin)
    return outs


def _project(x, gat, win, gq, gk, gmat, cos, sin, aw):
    ms = jnp.mean(x * x, axis=-1, keepdims=True)
    n = (x * lax.rsqrt(ms + EPS) * gat).astype(BF16)
    z = jnp.dot(n, win, preferred_element_type=F32)
    q_blocks = _group_rms_rope(z[:, 0:aw], gq, gmat, cos, sin)
    k_blocks = _group_rms_rope(z[:, aw:2 * aw], gk, gmat, cos, sin)
    v = z[:, 2 * aw:3 * aw]
    cw = (z.shape[1] - 3 * aw) // 3
    gb = z[:, 3 * aw:3 * aw + cw]
    gc = z[:, 3 * aw + cw:3 * aw + 2 * cw]
    h = z[:, 3 * aw + 2 * cw:]
    return q_blocks, k_blocks, v, gb, gc * h


def _head_norm_rope_t(zt, gain, cos, sin):
    hd = zt.shape[0] // 2
    rot = hd // 2
    t = zt.shape[1]
    halves = []
    for c in range(2):
        x = zt[c * hd:(c + 1) * hd, :]
        ms = jnp.mean(x * x, axis=0, keepdims=True)
        halves.append(x * lax.rsqrt(ms + EPS))
    zn = jnp.concatenate(halves, axis=0)
    zn = jnp.concatenate([zn[:, j * LANES:(j + 1) * LANES] * gain for j in range(t // LANES)],
                         axis=1)
    out = []
    for c in range(2):
        x1 = zn[c * hd:c * hd + rot, :]
        x2 = zn[c * hd + rot:(c + 1) * hd, :]
        out += [x1 * cos - x2 * sin, x2 * cos + x1 * sin]
    return jnp.concatenate(out, axis=0)


def _inproj_prompt_body(x_ref, cos_ref, sin_ref, gat_ref, gq_ref, gk_ref, win_ref,
                        wconv_ref, qt_ref, kt_ref, v_ref, kb_ref, vt_ref, gbc_ref, convp_ref,
                        ubuf, *, tm, aw, scale):
    j = pl.program_id(1)
    nh = aw // LANES
    x = x_ref[...]
    ms = jnp.mean(x * x, axis=-1, keepdims=True)
    n = (x * lax.rsqrt(ms + EPS) * gat_ref[...]).astype(BF16)
    cw = (win_ref.shape[1] - 3 * aw) // 3

    def project(lo, hi):
        return jnp.dot(n, win_ref[:, lo:hi], preferred_element_type=F32)

    cos = cos_ref[...]
    sin = sin_ref[...]
    zq = project(0, aw)
    for h in range(nh):
        qt = _head_norm_rope_t(zq[:, h * LANES:(h + 1) * LANES].T, gq_ref[...], cos, sin)
        qt_ref[h] = (qt * scale).astype(BF16)
    zk = project(aw, 2 * aw)
    for h in range(nh):
        cols = slice(h * LANES, (h + 1) * LANES)
        kt = _head_norm_rope_t(zk[:, cols].T, gk_ref[...], cos, sin)
        kt_ref[cols, :] = kt
        kb_ref[h] = kt.T.astype(BF16)
    zv = project(2 * aw, 3 * aw)
    for h in range(nh):
        vh = zv[:, h * LANES:(h + 1) * LANES]
        vt_ref[h] = vh.T.astype(BF16)
        v_ref[pl.ds(h, tm, stride=nh), :] = vh
    gb = project(3 * aw, 3 * aw + cw)
    u = project(3 * aw + cw, 3 * aw + 2 * cw) * project(3 * aw + 2 * cw, 3 * aw + 3 * cw)

    @pl.when(j == 0)
    def _():
        ubuf[0:SUBLANES, :] = jnp.zeros((SUBLANES, ubuf.shape[1]), F32)

    ubuf[SUBLANES:SUBLANES + tm, :] = u
    w = wconv_ref[...]
    c = (ubuf[SUBLANES - 2:SUBLANES - 2 + tm, :] * w[0:1, :]
         + ubuf[SUBLANES - 1:SUBLANES - 1 + tm, :] * w[1:2, :]
         + u * w[2:3, :])
    gbc_ref[...] = (gb * c).astype(BF16)
    convp_ref[...] = ubuf[SUBLANES + tm - 2:SUBLANES + tm, :]
    ubuf[0:SUBLANES, :] = ubuf[tm:tm + SUBLANES, :]


def _inproj_prompt(x, cos, sin, gat, gq, gk, win, wconv, *, tm):
    b, s, d = x.shape
    cw = wconv.shape[1]
    pw = win.shape[1]
    aw = (pw - 3 * cw) // 3
    nh = aw // LANES
    rot = cos.shape[0]
    grid = (b, s // tm)
    body = functools.partial(_inproj_prompt_body, tm=tm, aw=aw,
                             scale=(LANES // 2) ** -0.5 * math.log2(math.e))
    return pl.pallas_call(
        body,
        grid=grid,
        in_specs=[
            pl.BlockSpec((None, tm, d), lambda i, j: (i, j, 0)),
            pl.BlockSpec((rot, tm), lambda i, j: (0, j)),
            pl.BlockSpec((rot, tm), lambda i, j: (0, j)),
            _const_spec((1, d), 2),
            _const_spec((LANES, LANES), 2),
            _const_spec((LANES, LANES), 2),
            _const_spec((d, pw), 2),
            _const_spec((3, cw), 2),
        ],
        out_specs=[
            pl.BlockSpec((None, nh, LANES, tm), lambda i, j: (i, 0, 0, j)),
            pl.BlockSpec((None, aw, tm), lambda i, j: (i, 0, j)),
            pl.BlockSpec((None, tm * nh, LANES), lambda i, j: (i, j, 0)),
            pl.BlockSpec((None, nh, tm, LANES), lambda i, j: (i, 0, j, 0)),
            pl.BlockSpec((None, nh, LANES, tm), lambda i, j: (i, 0, 0, j)),
            pl.BlockSpec((None, tm, cw), lambda i, j: (i, j, 0)),
            pl.BlockSpec((None, 2, cw), lambda i, j: (i, 0, 0)),
        ],
        out_shape=[
            jax.ShapeDtypeStruct((b, nh, LANES, s), BF16),
            jax.ShapeDtypeStruct((b, aw, s), F32),
            jax.ShapeDtypeStruct((b, s * nh, LANES), F32),
            jax.ShapeDtypeStruct((b, nh, s, LANES), BF16),
            jax.ShapeDtypeStruct((b, nh, LANES, s), BF16),
            jax.ShapeDtypeStruct((b, s, cw), BF16),
            jax.ShapeDtypeStruct((b, 2, cw), F32),
        ],
        scratch_shapes=[pltpu.VMEM((tm + SUBLANES, cw), F32)],
        compiler_params=pltpu.CompilerParams(
            dimension_semantics=("arbitrary", "arbitrary"),
            vmem_limit_bytes=VMEM_LIMIT_BYTES),
        name="inproj_prompt",
    )(x, cos, sin, gat, gq, gk, win, wconv)


def _inproj_decode_body(x_ref, cos_ref, sin_ref, gat_ref, gq_ref, gk_ref, win_ref, gmat_ref,
                        wconv_ref, st_ref, q_ref, k_ref, v_ref, gbc_ref, convs_ref, *, aw, scale):
    q_blocks, k_blocks, v, gb, u = _project(
        x_ref[...], gat_ref[...], win_ref[...], gq_ref[...], gk_ref[...], gmat_ref[...],
        cos_ref[...], sin_ref[...], aw)
    for h, (qb, kb) in enumerate(zip(q_blocks, k_blocks)):
        q_ref[:, h * LANES:(h + 1) * LANES] = (qb * scale).astype(BF16)
        k_ref[:, h * LANES:(h + 1) * LANES] = kb
    v_ref[...] = v
    cw = u.shape[1]
    w = wconv_ref[...]
    s0 = st_ref[:, 0:cw]
    s1 = st_ref[:, cw:2 * cw]
    c = s0 * w[0:1, :] + s1 * w[1:2, :] + u * w[2:3, :]
    gbc_ref[...] = (gb * c).astype(BF16)
    convs_ref[:, 0:cw] = s1
    convs_ref[:, cw:2 * cw] = u


def _inproj_decode(x, cos, sin, gat, gq, gk, win, gmat, wconv, state):
    n, d = x.shape
    aw = gmat.shape[0]
    cw = wconv.shape[1]
    nh = aw // LANES
    body = functools.partial(_inproj_decode_body, aw=aw, scale=(LANES // 2) ** -0.5)
    full = lambda shape: pl.BlockSpec(shape, lambda i: (0,) * len(shape))
    return pl.pallas_call(
        body,
        grid=(1,),
        in_specs=[full((n, d)), full((1, LANES)), full((1, LANES)), full((1, d)), full((1, aw)),
                  full((1, aw)), full(win.shape), full((aw, aw)), full((3, cw)), full((n, 2 * cw))],
        out_specs=[full((n, aw)), full((n, aw)), full((n, aw)), full((n, cw)), full((n, 2 * cw))],
        out_shape=[
            jax.ShapeDtypeStruct((n, aw), BF16),
            jax.ShapeDtypeStruct((n, aw), F32),
            jax.ShapeDtypeStruct((n, aw), F32),
            jax.ShapeDtypeStruct((n, cw), BF16),
            jax.ShapeDtypeStruct((n, 2 * cw), F32),
        ],
        compiler_params=pltpu.CompilerParams(
            dimension_semantics=("arbitrary",), vmem_limit_bytes=VMEM_LIMIT_BYTES),
        name="inproj_decode",
    )(x, cos, sin, gat, gq, gk, win, gmat, wconv, state)


def _subln(o, gsub, post_scale):
    ms = jnp.mean(o * o, axis=-1, keepdims=True)
    return o * lax.rsqrt(ms + EPS) * gsub * post_scale


def _attn_body(pt_ref, lq1_ref, lk1_ref, lq2_ref, lk2_ref, gcol_ref, grow_ref, exp_ref,
               qt_ref, k_ref, vt_ref, qd_ref, kn_ref, vn_ref, ck_hbm, cv_hbm,
               o_ref, od_ref,
               qz_scr, s_scr, mb_scr, m_scr, l_scr, acc_scr, kbuf, vbuf, ksem, vsem, dstate,
               sdec, smax, *, tq, tk, n_pages, cp, spp, lam_init):
    assert tq == 2 * tk
    i = pl.program_id(2)
    step = (pl.program_id(0) * pl.num_programs(1) + pl.program_id(1)) * pl.num_programs(2) + i
    n_steps = pl.num_programs(0) * pl.num_programs(1) * pl.num_programs(2)
    nc = n_pages // cp
    per_step = spp * nc
    nh = od_ref.shape[0]
    lam = _lam_value(lq1_ref[...], lk1_ref[...], lq2_ref[...], lk2_ref[...], lam_init)

    n_slots = kbuf.shape[0]
    assert per_step % n_slots == 0

    def copies(at_step, pos, which):
        first_page = (at_step * spp + pos // nc) * n_pages + (pos % nc) * cp
        hbm, buf, sem = ((ck_hbm, kbuf, ksem), (cv_hbm, vbuf, vsem))[which]
        slot = pos % n_slots
        return [pltpu.make_async_copy(hbm.at[pt_ref[first_page + p]], buf.at[slot, p], sem.at[slot])
                for p in range(cp)]

    def start_all(at_step, jobs):
        for pos, which in jobs:
            for cpy in copies(at_step, pos, which):
                cpy.start()

    def decode_point(point, prompt_work=lambda: None):
        values = [pos for pos in range(per_step) if (pos * 4) // per_step == point]
        scores = [pos + 1 for pos in values if pos + 1 < per_step]
        if point == 0:
            scores = [0] + scores
        jobs = [(pos, 0) for pos in scores] + [(pos, 1) for pos in values]
        for pos, which in jobs:
            for cpy in copies(step, pos, which):
                cpy.wait()
        for pos in scores:
            _decode_scores(qd_ref[pos // nc], kbuf.at[pos % n_slots], sdec.at[pos % 2],
                           smax.at[pos % 2])
        state = None
        for pos in values:
            t, c = pos // nc, pos % nc
            if c == 0:
                state = _decode_new_token(qd_ref[t], kn_ref[t], vn_ref[t])
            elif state is None:
                state = (dstate[0], dstate[1], dstate[2])
            state = _decode_values(sdec.at[pos % 2], smax.at[pos % 2], vbuf.at[pos % n_slots],
                                   exp_ref[...], state)
            if c == nc - 1:
                a = _decode_output(state, lam, grow_ref[...], lam_init)
                for h in range(nh):
                    od_ref[h, t] = a[2 * h:2 * h + 1, :].astype(BF16)
                state = None
        if state is not None:
            for n, v in enumerate(state):
                dstate[n] = v
        prompt_work()
        refill = [(pos + n_slots, which) for pos, which in jobs]
        start_all(step, [(pos, which) for pos, which in refill if pos < per_step])
        wrapped = [(pos - per_step, which) for pos, which in refill if pos >= per_step]
        if wrapped:
            @pl.when(step + 1 < n_steps)
            def _():
                start_all(step + 1, wrapped)

    @pl.when(step == 0)
    def _():
        start_all(0, [(pos, which) for pos in range(n_slots) for which in (0, 1)])

    n2 = 2 * tq
    all_cols = tuple((c, c + ATTN_COL_BLOCK) for c in range(0, n2, ATTN_COL_BLOCK))
    late_cols = ((tq // 2, tq), (tq + tq // 2, n2))

    def stage_a(start, buf, diag_offset, cols=all_cols):
        kblk = k_ref[pl.ds(pl.multiple_of(start, tk), tk), :]
        for lo, hi in cols:
            s = jnp.dot(kblk, qz_scr[:, lo:hi], preferred_element_type=F32)
            if diag_offset is not None:
                key = lax.broadcasted_iota(jnp.int32, s.shape, 0) + diag_offset
                col = lax.broadcasted_iota(jnp.int32, s.shape, 1) + lo
                qry = jnp.where(col >= tq, col - tq, col)
                s = jnp.where(key <= qry, s, NEG)
            s_scr[buf, :, lo:hi] = s
            mb_scr[buf, :, lo:hi] = jnp.max(s, axis=0, keepdims=True)

    def stage_b(start, buf, cols=all_cols):
        vt = vt_ref[:, pl.ds(pl.multiple_of(start, tk), tk)]
        for lo, hi in cols:
            m_prev = m_scr[:, lo:hi]
            m_new = jnp.maximum(m_prev, mb_scr[buf, :, lo:hi])
            alpha = jnp.exp2(m_prev - m_new)
            p = jnp.exp2(s_scr[buf, :, lo:hi] - m_new)
            l_scr[:, lo:hi] = alpha * l_scr[:, lo:hi] + jnp.sum(p, axis=0, keepdims=True)
            acc_scr[:, lo:hi] = alpha * acc_scr[:, lo:hi] + jnp.dot(vt, p.astype(BF16),
                                                                    preferred_element_type=F32)
            m_scr[:, lo:hi] = m_new

    diag0 = i * tq

    def prologue():
        qt = qt_ref[...]
        sub = lax.broadcasted_iota(jnp.int32, qt.shape, 0)
        zero = jnp.zeros_like(qt)
        qz_scr[:, 0:tq] = jnp.where(sub < LANES // 2, qt, zero)
        qz_scr[:, tq:n2] = jnp.where(sub >= LANES // 2, qt, zero)
        m_scr[...] = jnp.full(m_scr.shape, NEG, F32)
        l_scr[...] = jnp.zeros(l_scr.shape, F32)
        acc_scr[...] = jnp.zeros(acc_scr.shape, F32)
        stage_a(diag0, 0, 0)

    def epilogue():
        before = jnp.where(i == 0, diag0, diag0 - tk)
        for n, c in enumerate(all_cols):
            if n < len(late_cols):
                stage_a(diag0 + tk, 1, tk, (late_cols[n],))
            stage_b(before, 0, (c,))
        stage_b(diag0 + tk, 1, late_cols)
        ot = (acc_scr[:, 0:tq] / l_scr[:, 0:tq]
              - lam * (acc_scr[:, tq:n2] / l_scr[:, tq:n2]))
        ms = jnp.mean(ot * ot, axis=0, keepdims=True)
        at = ot * lax.rsqrt(ms + EPS) * gcol_ref[...] * (1.0 - lam_init)
        o_ref[...] = at.T.astype(BF16)

    decode_point(0, prologue)
    decode_point(1)

    def pair(u, carry):
        first = 2 * u * tk
        before = jnp.where(u == 0, diag0, first - tk)
        for c in all_cols:
            stage_a(first, 1, None, (c,))
            stage_b(before, 0, (c,))
        for c in all_cols:
            stage_a(first + tk, 0, None, (c,))
            stage_b(first, 1, (c,))
        return carry

    lax.fori_loop(0, i, pair, 0)
    decode_point(2)
    decode_point(3, epilogue)


def _attention(pt_flat, lams, gsub, expand, qt, k, vt, qd, kn, vn, ck, cv,
               *, tq, tk, n_pages, cp, lam_init):
    b, nh, s, _ = k.shape
    n, _, aw = qd.shape
    hd = lams[0].shape[1]
    nq = s // tq
    n_steps = b * nh * nq
    assert n % n_steps == 0, "decode samples must split evenly over the attention grid steps"
    spp = n // n_steps
    per_step = spp * (n_pages // cp)
    n_slots = max(c for c in (2, 4, 8) if per_step % c == 0)
    body = functools.partial(_attn_body, tq=tq, tk=tk, n_pages=n_pages, cp=cp, spp=spp,
                             lam_init=lam_init)
    const = lambda shape: pl.BlockSpec(shape, lambda bi, h, i, pt: (0,) * len(shape))
    sample_rows = pl.BlockSpec((spp, 1, aw), lambda bi, h, i, pt: ((bi * nh + h) * nq + i, 0, 0))
    grid_spec = pltpu.PrefetchScalarGridSpec(
        num_scalar_prefetch=1,
        grid=(b, nh, nq),
        in_specs=[const((1, hd))] * 4 + [
            const((LANES, 1)), const((1, LANES)), const(expand.shape),
            pl.BlockSpec((None, None, LANES, tq), lambda bi, h, i, pt: (bi, h, 0, i)),
            pl.BlockSpec((None, None, s, LANES), lambda bi, h, i, pt: (bi, h, 0, 0)),
            pl.BlockSpec((None, None, LANES, s), lambda bi, h, i, pt: (bi, h, 0, 0)),
            sample_rows, sample_rows, sample_rows,
            pl.BlockSpec(memory_space=pl.ANY),
            pl.BlockSpec(memory_space=pl.ANY),
        ],
        out_specs=[
            pl.BlockSpec((None, None, tq, LANES), lambda bi, h, i, pt: (bi, h, i, 0)),
            pl.BlockSpec((nh, spp, 1, LANES), lambda bi, h, i, pt: (0, (bi * nh + h) * nq + i, 0, 0)),
        ],
        scratch_shapes=[
            pltpu.VMEM((LANES, 2 * tq), BF16),
            pltpu.VMEM((2, tk, 2 * tq), F32),
            pltpu.VMEM((2, 1, 2 * tq), F32),
            pltpu.VMEM((1, 2 * tq), F32),
            pltpu.VMEM((1, 2 * tq), F32),
            pltpu.VMEM((LANES, 2 * tq), F32),
            pltpu.VMEM((n_slots, cp) + ck.shape[1:], F32),
            pltpu.VMEM((n_slots, cp) + cv.shape[1:], F32),
            pltpu.SemaphoreType.DMA((n_slots,)),
            pltpu.SemaphoreType.DMA((n_slots,)),
            pltpu.VMEM((3, SUBLANES, LANES), F32),
            pltpu.VMEM((2, cp * SUBLANES, LANES), F32),
            pltpu.VMEM((2, SUBLANES, LANES), F32),
        ],
    )
    return pl.pallas_call(
        body,
        grid_spec=grid_spec,
        out_shape=[jax.ShapeDtypeStruct((b, nh, s, LANES), BF16),
                   jax.ShapeDtypeStruct((nh, n, 1, LANES), BF16)],
        compiler_params=pltpu.CompilerParams(
            dimension_semantics=("arbitrary", "arbitrary", "arbitrary"),
            vmem_limit_bytes=VMEM_LIMIT_BYTES),
        name="attention",
    )(pt_flat, *lams, gsub.reshape(LANES, 1), gsub, expand, qt, k, vt, qd, kn, vn, ck, cv)


def _decode_query_rows(q):
    aw = q.shape[-1]
    row = lax.broadcasted_iota(jnp.int32, (SUBLANES, aw), 0)
    lane = lax.broadcasted_iota(jnp.int32, (SUBLANES, aw), 1)
    return jnp.where(lane // (LANES // 2) == row, q.astype(F32), 0.0)


def _decode_new_token(q, kn, vn):
    nh = q.shape[-1] // LANES
    m = jnp.sum(_decode_query_rows(q) * kn, axis=1, keepdims=True)
    m = jnp.broadcast_to(m, (SUBLANES, LANES))
    l = jnp.ones((SUBLANES, LANES), F32)
    row_v = lax.broadcasted_iota(jnp.int32, (SUBLANES, LANES), 0)
    acc = jnp.zeros((SUBLANES, LANES), F32)
    for h in range(nh):
        acc = jnp.where(row_v // 2 == h, vn[:, h * LANES:(h + 1) * LANES], acc)
    return m, l, acc


def _decode_scores(q, kpages, s_out, smax_out):
    cp = kpages.shape[0]
    qbd_b = _decode_query_rows(q).astype(BF16)
    s = [jnp.dot(qbd_b, kpages[p].astype(BF16), preferred_element_type=F32)
         for p in range(cp)]
    for p in range(cp):
        s_out[p * SUBLANES:(p + 1) * SUBLANES, :] = s[p]
    smax_out[...] = functools.reduce(jnp.maximum, s)


def _decode_values(s_in, smax_in, vpages, expand, state):
    m, l, acc = state
    cp = vpages.shape[0]
    nh = vpages.shape[1] // PAGE_SIZE
    m_new = jnp.maximum(m, jnp.max(smax_in[...], axis=1, keepdims=True))
    alpha = jnp.exp(m - m_new)
    p3 = jnp.exp(s_in[...].reshape(cp, SUBLANES, LANES) - m_new[None])
    l = alpha * l + jnp.sum(jnp.sum(p3, axis=0), axis=1, keepdims=True)
    p_all = p3.reshape(cp * SUBLANES, LANES)
    w_all = jnp.dot(p_all.astype(BF16), expand, preferred_element_type=F32)
    row_e = lax.broadcasted_iota(jnp.int32, (SUBLANES, nh * PAGE_SIZE), 0)
    lane_e = lax.broadcasted_iota(jnp.int32, (SUBLANES, nh * PAGE_SIZE), 1)
    own_head = lane_e % nh == row_e // 2
    pv = jnp.zeros((SUBLANES, LANES), F32)
    for p in range(cp):
        w_p = jnp.where(own_head, w_all[p * SUBLANES:(p + 1) * SUBLANES, :], 0.0).astype(BF16)
        pv = pv + jnp.dot(w_p, vpages[p].astype(BF16), preferred_element_type=F32)
    return m_new, l, alpha * acc + pv


def _decode_output(state, lam, gsub_row, lam_init):
    _, l, acc = state
    row_v = lax.broadcasted_iota(jnp.int32, (SUBLANES, LANES), 0)
    t = (acc / l) * jnp.where(row_v % 2 == 0, 1.0, -lam)
    o = t + pltpu.roll(t, SUBLANES - 1, 0)
    return _subln(o, gsub_row, 1.0 - lam_init)


def _finish_body(x_ref, a_ref, gbc_ref, wout_ref, gmlp_ref, wup_ref, wdown_ref, y_ref, *, fc):
    nh = a_ref.shape[0]
    mix = jnp.concatenate([a_ref[h] for h in range(nh)] + [gbc_ref[...]], axis=1)
    h = x_ref[...] + jnp.dot(mix, wout_ref[...], preferred_element_type=F32)
    ms = jnp.mean(h * h, axis=-1, keepdims=True)
    hn = (h * lax.rsqrt(ms + EPS) * gmlp_ref[...]).astype(BF16)
    y = h
    dff = wup_ref.shape[1]
    for c in range(dff // fc):
        z = jnp.dot(hn, wup_ref[:, c * fc:(c + 1) * fc], preferred_element_type=F32)
        r = jnp.maximum(z, 0.0)
        y = y + jnp.dot((r * r).astype(BF16), wdown_ref[c * fc:(c + 1) * fc, :],
                        preferred_element_type=F32)
    y_ref[...] = y


def _finish(x, a, gbc, wout, gmlp, wup, wdown, *, tm):
    b, s, d = x.shape
    nh = a.shape[1]
    cw = gbc.shape[2]
    dff = wup.shape[1]
    body = functools.partial(_finish_body, fc=_pick(dff, 1024))
    return pl.pallas_call(
        body,
        grid=(b, s // tm),
        in_specs=[
            pl.BlockSpec((None, tm, d), lambda i, j: (i, j, 0)),
            pl.BlockSpec((None, nh, tm, LANES), lambda i, j: (i, 0, j, 0)),
            pl.BlockSpec((None, tm, cw), lambda i, j: (i, j, 0)),
            _const_spec((d, d), 2),
            _const_spec((1, d), 2),
            _const_spec((d, dff), 2),
            _const_spec((dff, d), 2),
        ],
        out_specs=pl.BlockSpec((None, tm, d), lambda i, j: (i, j, 0)),
        out_shape=jax.ShapeDtypeStruct((b, s, d), F32),
        compiler_params=pltpu.CompilerParams(
            dimension_semantics=("arbitrary", "arbitrary"),
            vmem_limit_bytes=VMEM_LIMIT_BYTES),
        name="finish",
    )(x, a, gbc, wout, gmlp, wup, wdown)


def _rope_angles(pos, half):
    inv = ROPE_THETA ** (-jnp.arange(half, dtype=F32) / half)
    return pos.astype(F32)[:, None] * inv[None, :]


def _rope_tables(pos, half):
    ang = _rope_angles(pos, half)
    cos = jnp.cos(ang)
    sin = jnp.sin(ang)
    return (jnp.concatenate([cos, cos, cos, cos], axis=1),
            jnp.concatenate([-sin, sin, -sin, sin], axis=1))


def kernel(x_prompt, x_sample, cache_k, cache_v, state_conv, page_table, g_attn_norm, w_in, g_q,
           g_k, lam_q1, lam_k1, lam_q2, lam_k2, g_subln, w_conv, w_out, g_mlp_norm, w_up, w_down):
    b, s, d = x_prompt.shape
    db, dec_seq, _ = x_sample.shape
    assert dec_seq == 1, "decode path handles one new token per sample"
    depth = w_in.shape[0]
    hd = g_q.shape[-1]
    vd = g_subln.shape[-1]
    assert vd == LANES and 2 * hd == LANES, "one head must fill one 128-lane block"
    aw = d // 2
    cw = d - aw
    nh = aw // vd
    n_pool = cache_k.shape[1]
    n_pages = page_table.shape[1]
    past = n_pages * PAGE_SIZE
    assert cache_k.shape[2] == PAGE_SIZE

    tm = _pick(s, 1024)
    assert s % 2 == 0
    tk = _pick(s // 2, 512)
    tq = 2 * tk
    assert n_pages % 2 == 0
    cp = _pick(n_pages // 2, 16)

    ang_p = _rope_angles(jnp.arange(s), hd // 2).T
    cos_p, sin_p = jnp.cos(ang_p), jnp.sin(ang_p)
    cos_s, sin_s = _rope_tables(past + jnp.arange(1), hd // 2)
    grp = jnp.arange(aw) // hd
    gmat = ((grp[:, None] == grp[None, :]).astype(F32) / hd).astype(BF16)

    ck = jnp.transpose(cache_k, (0, 1, 3, 4, 5, 2)).reshape(depth * n_pool, aw, PAGE_SIZE)
    cv = cache_v.reshape(depth * n_pool, PAGE_SIZE * nh, vd)
    slot_of = jnp.arange(PAGE_SIZE * nh) // nh
    expand = (jnp.arange(PAGE_SIZE)[:, None] == slot_of[None, :]).astype(BF16)

    xp = x_prompt
    xs = x_sample.reshape(db, d)
    kp, vp, cpr, ksl, vsl, csl = [], [], [], [], [], []
    for l in range(depth):
        lam_init = 0.8 - 0.6 * math.exp(-0.3 * l)
        lams = tuple(v[l].reshape(1, hd).astype(F32) for v in (lam_q1, lam_k1, lam_q2, lam_k2))
        gat = g_attn_norm[l].reshape(1, d)
        gq = jnp.tile(g_q[l], aw // hd).reshape(1, aw)
        gk = jnp.tile(g_k[l], aw // hd).reshape(1, aw)
        gsub = g_subln[l].reshape(1, vd)
        gmlp = g_mlp_norm[l].reshape(1, d)
        win = w_in[l].astype(BF16)
        wout = w_out[l].astype(BF16)
        wup = w_up[l].astype(BF16)
        wdown = w_down[l].astype(BF16)
        wconv = w_conv[l]

        gq_rows = jnp.broadcast_to(jnp.tile(g_q[l], 2)[:, None], (LANES, LANES))
        gk_rows = jnp.broadcast_to(jnp.tile(g_k[l], 2)[:, None], (LANES, LANES))
        qt, kt, v, kb, vt, gbc, convp = _inproj_prompt(xp, cos_p, sin_p, gat, gq_rows, gk_rows,
                                                       win, wconv, tm=tm)
        qd, kn, vn, gbcd, convs = _inproj_decode(xs, cos_s, sin_s, gat, gq, gk, win, gmat, wconv,
                                                 state_conv[l].reshape(db, 2 * cw))
        pt_flat = (page_table + l * n_pool).reshape(-1).astype(jnp.int32)
        a, ad = _attention(pt_flat, lams, gsub, expand, qt, kb, vt, qd.reshape(db, 1, aw),
                           kn.reshape(db, 1, aw), vn.reshape(db, 1, aw), ck, cv,
                           tq=tq, tk=tk, n_pages=n_pages, cp=cp, lam_init=lam_init)
        xp = _finish(xp, a, gbc, wout, gmlp, wup, wdown, tm=tm)
        kp.append(jnp.transpose(kt.reshape(b, nh, 2, hd, s), (0, 4, 1, 2, 3)))
        vp.append(v.reshape(b, s, nh, vd))
        cpr.append(convp)
        xs = _finish(xs.reshape(1, db, d), ad.reshape(1, nh, db, LANES), gbcd.reshape(1, db, cw),
                     wout, gmlp, wup, wdown, tm=db).reshape(db, d)
        ksl.append(kn.reshape(db, 1, nh, 2, hd))
        vsl.append(vn.reshape(db, 1, nh, vd))
        csl.append(convs.reshape(db, 2, cw))

    return (xp, xs.reshape(db, 1, d), jnp.stack(kp), jnp.stack(vp), jnp.stack(cpr),
            jnp.stack(ksl), jnp.stack(vsl), jnp.stack(csl))
```

```python
import functools
import math

import jax
import jax.numpy as jnp
from jax import lax
from jax.experimental import pallas as pl
from jax.experimental.pallas import tpu as pltpu

F32 = jnp.float32
BF16 = jnp.bfloat16

PAGE_SIZE = 128
ROPE_THETA = 10000.0
EPS = 1e-6
NEG = -1e30

LANES = 128
SUBLANES = 8
VMEM_LIMIT_BYTES = 60 * 1024 * 1024
ATTN_COL_BLOCK = 512
ONES_ROWS = 16


def _pick(n, target):
    t = min(n, target)
    while n % t:
        t -= 1
    return t


def _const_spec(shape, n_grid):
    zeros = (0,) * len(shape)
    if n_grid == 1:
        return pl.BlockSpec(shape, lambda a: zeros, pipeline_mode=pl.Buffered(1))
    if n_grid == 2:
        return pl.BlockSpec(shape, lambda a, b: zeros, pipeline_mode=pl.Buffered(1))
    return pl.BlockSpec(shape, lambda a, b, c: zeros, pipeline_mode=pl.Buffered(1))


def _lam_value(lq1, lk1, lq2, lk2, lam_init):
    s1 = jnp.sum(lq1 * lk1, axis=-1, keepdims=True)
    s2 = jnp.sum(lq2 * lk2, axis=-1, keepdims=True)
    return jnp.exp(s1) - jnp.exp(s2) + lam_init


def _group_rms_rope(z, gain, gmat, cos, sin):
    zz = z * z
    hi = zz.astype(BF16)
    lo = (zz - hi.astype(F32)).astype(BF16)
    g = gmat
    ms = (jnp.dot(hi, g, preferred_element_type=F32)
          + jnp.dot(lo, g, preferred_element_type=F32))
    zn = z * lax.rsqrt(ms + EPS) * gain
    t = z.shape[0]
    lane = lax.broadcasted_iota(jnp.int32, (t, LANES), 1)
    upper = (lane & 32) != 0
    outs = []
    for c in range(z.shape[1] // LANES):
        blk = zn[:, c * LANES:(c + 1) * LANES]
        swapped = jnp.where(upper, pltpu.roll(blk, 32, 1), pltpu.roll(blk, LANES - 32, 1))
        outs.append(blk * cos + swapped * sin)
    return outs


def _project(x, gat, win, gq, gk, gmat, cos, sin, aw):
    ms = jnp.mean(x * x, axis=-1, keepdims=True)
    n = (x * lax.rsqrt(ms + EPS) * gat).astype(BF16)
    z = jnp.dot(n, win, preferred_element_type=F32)
    q_blocks = _group_rms_rope(z[:, 0:aw], gq, gmat, cos, sin)
    k_blocks = _group_rms_rope(z[:, aw:2 * aw], gk, gmat, cos, sin)
    v = z[:, 2 * aw:3 * aw]
    cw = (z.shape[1] - 3 * aw) // 3
    gb = z[:, 3 * aw:3 * aw + cw]
    gc = z[:, 3 * aw + cw:3 * aw + 2 * cw]
    h = z[:, 3 * aw + 2 * cw:]
    return q_blocks, k_blocks, v, gb, gc * h


def _head_norm_rope_t(zt, gain, cos, sin):
    hd = zt.shape[0] // 2
    rot = hd // 2
    t = zt.shape[1]
    halves = []
    for c in range(2):
        x = zt[c * hd:(c + 1) * hd, :]
        ms = jnp.mean(x * x, axis=0, keepdims=True)
        halves.append(x * lax.rsqrt(ms + EPS))
    zn = jnp.concatenate(halves, axis=0)
    zn = jnp.concatenate([zn[:, j * LANES:(j + 1) * LANES] * gain for j in range(t // LANES)],
                         axis=1)
    out = []
    for c in range(2):
        x1 = zn[c * hd:c * hd + rot, :]
        x2 = zn[c * hd + rot:(c + 1) * hd, :]
        out += [x1 * cos - x2 * sin, x2 * cos + x1 * sin]
    return jnp.concatenate(out, axis=0)


def _inproj_prompt_body(x_ref, cos_ref, sin_ref, gat_ref, gq_ref, gk_ref, win_ref,
                        wconv_ref, qt_ref, kt_ref, v_ref, kb_ref, vt_ref, gbc_ref, convp_ref,
                        ubuf, *, tm, aw, scale):
    j = pl.program_id(1)
    nh = aw // LANES
    x = x_ref[...]
    ms = jnp.mean(x * x, axis=-1, keepdims=True)
    n = (x * lax.rsqrt(ms + EPS) * gat_ref[...]).astype(BF16)
    cw = (win_ref.shape[1] - 3 * aw) // 3

    def project(lo, hi):
        return jnp.dot(n, win_ref[:, lo:hi], preferred_element_type=F32)

    cos = cos_ref[...]
    sin = sin_ref[...]
    zq = project(0, aw)
    for h in range(nh):
        qt = _head_norm_rope_t(zq[:, h * LANES:(h + 1) * LANES].T, gq_ref[...], cos, sin)
        qt_ref[h] = (qt * scale).astype(BF16)
    zk = project(aw, 2 * aw)
    for h in range(nh):
        cols = slice(h * LANES, (h + 1) * LANES)
        kt = _head_norm_rope_t(zk[:, cols].T, gk_ref[...], cos, sin)
        kt_ref[cols, :] = kt
        kb_ref[h] = kt.T.astype(BF16)
    zv = project(2 * aw, 3 * aw)
    for h in range(nh):
        vh = zv[:, h * LANES:(h + 1) * LANES]
        vt_ref[h] = vh.T.astype(BF16)
        v_ref[pl.ds(h, tm, stride=nh), :] = vh
    gb = project(3 * aw, 3 * aw + cw)
    u = project(3 * aw + cw, 3 * aw + 2 * cw) * project(3 * aw + 2 * cw, 3 * aw + 3 * cw)

    @pl.when(j == 0)
    def _():
        ubuf[0:SUBLANES, :] = jnp.zeros((SUBLANES, ubuf.shape[1]), F32)

    ubuf[SUBLANES:SUBLANES + tm, :] = u
    w = wconv_ref[...]
    c = (ubuf[SUBLANES - 2:SUBLANES - 2 + tm, :] * w[0:1, :]
         + ubuf[SUBLANES - 1:SUBLANES - 1 + tm, :] * w[1:2, :]
         + u * w[2:3, :])
    gbc_ref[...] = (gb * c).astype(BF16)
    convp_ref[...] = ubuf[SUBLANES + tm - 2:SUBLANES + tm, :]
    ubuf[0:SUBLANES, :] = ubuf[tm:tm + SUBLANES, :]


def _inproj_prompt(x, cos, sin, gat, gq, gk, win, wconv, *, tm):
    b, s, d = x.shape
    cw = wconv.shape[1]
    pw = win.shape[1]
    aw = (pw - 3 * cw) // 3
    nh = aw // LANES
    rot = cos.shape[0]
    grid = (b, s // tm)
    body = functools.partial(_inproj_prompt_body, tm=tm, aw=aw,
                             scale=(LANES // 2) ** -0.5 * math.log2(math.e))
    return pl.pallas_call(
        body,
        grid=grid,
        in_specs=[
            pl.BlockSpec((None, tm, d), lambda i, j: (i, j, 0)),
            pl.BlockSpec((rot, tm), lambda i, j: (0, j)),
            pl.BlockSpec((rot, tm), lambda i, j: (0, j)),
            _const_spec((1, d), 2),
            _const_spec((LANES, LANES), 2),
            _const_spec((LANES, LANES), 2),
            _const_spec((d, pw), 2),
            _const_spec((3, cw), 2),
        ],
        out_specs=[
            pl.BlockSpec((None, nh, LANES, tm), lambda i, j: (i, 0, 0, j)),
            pl.BlockSpec((None, aw, tm), lambda i, j: (i, 0, j)),
            pl.BlockSpec((None, tm * nh, LANES), lambda i, j: (i, j, 0)),
            pl.BlockSpec((None, nh, tm, LANES), lambda i, j: (i, 0, j, 0)),
            pl.BlockSpec((None, nh, LANES, tm), lambda i, j: (i, 0, 0, j)),
            pl.BlockSpec((None, tm, cw), lambda i, j: (i, j, 0)),
            pl.BlockSpec((None, 2, cw), lambda i, j: (i, 0, 0)),
        ],
        out_shape=[
            jax.ShapeDtypeStruct((b, nh, LANES, s), BF16),
            jax.ShapeDtypeStruct((b, aw, s), F32),
            jax.ShapeDtypeStruct((b, s * nh, LANES), F32),
            jax.ShapeDtypeStruct((b, nh, s, LANES), BF16),
            jax.ShapeDtypeStruct((b, nh, LANES, s), BF16),
            jax.ShapeDtypeStruct((b, s, cw), BF16),
            jax.ShapeDtypeStruct((b, 2, cw), F32),
        ],
        scratch_shapes=[pltpu.VMEM((tm + SUBLANES, cw), F32)],
        compiler_params=pltpu.CompilerParams(
            dimension_semantics=("arbitrary", "arbitrary"),
            vmem_limit_bytes=VMEM_LIMIT_BYTES),
        name="inproj_prompt",
    )(x, cos, sin, gat, gq, gk, win, wconv)


def _inproj_decode_body(x_ref, cos_ref, sin_ref, gat_ref, gq_ref, gk_ref, win_ref, gmat_ref,
                        wconv_ref, st_ref, q_ref, k_ref, v_ref, gbc_ref, convs_ref, *, aw, scale):
    q_blocks, k_blocks, v, gb, u = _project(
        x_ref[...], gat_ref[...], win_ref[...], gq_ref[...], gk_ref[...], gmat_ref[...],
        cos_ref[...], sin_ref[...], aw)
    for h, (qb, kb) in enumerate(zip(q_blocks, k_blocks)):
        q_ref[:, h * LANES:(h + 1) * LANES] = (qb * scale).astype(BF16)
        k_ref[:, h * LANES:(h + 1) * LANES] = kb
    v_ref[...] = v
    cw = u.shape[1]
    w = wconv_ref[...]
    s0 = st_ref[:, 0:cw]
    s1 = st_ref[:, cw:2 * cw]
    c = s0 * w[0:1, :] + s1 * w[1:2, :] + u * w[2:3, :]
    gbc_ref[...] = (gb * c).astype(BF16)
    convs_ref[:, 0:cw] = s1
    convs_ref[:, cw:2 * cw] = u


def _inproj_decode(x, cos, sin, gat, gq, gk, win, gmat, wconv, state):
    n, d = x.shape
    aw = gmat.shape[0]
    cw = wconv.shape[1]
    nh = aw // LANES
    body = functools.partial(_inproj_decode_body, aw=aw, scale=(LANES // 2) ** -0.5)
    full = lambda shape: pl.BlockSpec(shape, lambda i: (0,) * len(shape))
    return pl.pallas_call(
        body,
        grid=(1,),
        in_specs=[full((n, d)), full((1, LANES)), full((1, LANES)), full((1, d)), full((1, aw)),
                  full((1, aw)), full(win.shape), full((aw, aw)), full((3, cw)), full((n, 2 * cw))],
        out_specs=[full((n, aw)), full((n, aw)), full((n, aw)), full((n, cw)), full((n, 2 * cw))],
        out_shape=[
            jax.ShapeDtypeStruct((n, aw), BF16),
            jax.ShapeDtypeStruct((n, aw), F32),
            jax.ShapeDtypeStruct((n, aw), F32),
            jax.ShapeDtypeStruct((n, cw), BF16),
            jax.ShapeDtypeStruct((n, 2 * cw), F32),
        ],
        compiler_params=pltpu.CompilerParams(
            dimension_semantics=("arbitrary",), vmem_limit_bytes=VMEM_LIMIT_BYTES),
        name="inproj_decode",
    )(x, cos, sin, gat, gq, gk, win, gmat, wconv, state)


def _subln(o, gsub, post_scale):
    ms = jnp.mean(o * o, axis=-1, keepdims=True)
    return o * lax.rsqrt(ms + EPS) * gsub * post_scale


def _attn_body(pt_ref, lq1_ref, lk1_ref, lq2_ref, lk2_ref, gcol_ref, grow_ref, exp_ref,
               qt_ref, k_ref, vt_ref, qd_ref, kn_ref, vn_ref, ck_hbm, cv_hbm,
               o_ref, od_ref,
               qz_scr, s_scr, mb_scr, m_scr, acc_scr, kbuf, vbuf, ksem, vsem, dstate,
               sdec, smax, *, tq, tk, n_pages, cp, spp, lam_init):
    assert tq == 2 * tk
    i = pl.program_id(2)
    step = (pl.program_id(0) * pl.num_programs(1) + pl.program_id(1)) * pl.num_programs(2) + i
    n_steps = pl.num_programs(0) * pl.num_programs(1) * pl.num_programs(2)
    nc = n_pages // cp
    per_step = spp * nc
    nh = od_ref.shape[0]
    lam = _lam_value(lq1_ref[...], lk1_ref[...], lq2_ref[...], lk2_ref[...], lam_init)

    n_slots = kbuf.shape[0]
    assert per_step % n_slots == 0

    def copies(at_step, pos, which):
        first_page = (at_step * spp + pos // nc) * n_pages + (pos % nc) * cp
        hbm, buf, sem = ((ck_hbm, kbuf, ksem), (cv_hbm, vbuf, vsem))[which]
        slot = pos % n_slots
        return [pltpu.make_async_copy(hbm.at[pt_ref[first_page + p]], buf.at[slot, p], sem.at[slot])
                for p in range(cp)]

    def start_all(at_step, jobs):
        for pos, which in jobs:
            for cpy in copies(at_step, pos, which):
                cpy.start()

    def decode_point(point, prompt_work=lambda: None):
        values = [pos for pos in range(per_step) if (pos * 4) // per_step == point]
        scores = [pos + 1 for pos in values if pos + 1 < per_step]
        if point == 0:
            scores = [0] + scores
        jobs = [(pos, 0) for pos in scores] + [(pos, 1) for pos in values]
        for pos, which in jobs:
            for cpy in copies(step, pos, which):
                cpy.wait()
        for pos in scores:
            _decode_scores(qd_ref[pos // nc], kbuf.at[pos % n_slots], sdec.at[pos % 2],
                           smax.at[pos % 2])
        state = None
        for pos in values:
            t, c = pos // nc, pos % nc
            if c == 0:
                state = _decode_new_token(qd_ref[t], kn_ref[t], vn_ref[t])
            elif state is None:
                state = (dstate[0], dstate[1], dstate[2])
            state = _decode_values(sdec.at[pos % 2], smax.at[pos % 2], vbuf.at[pos % n_slots],
                                   exp_ref[...], state)
            if c == nc - 1:
                a = _decode_output(state, lam, grow_ref[...], lam_init)
                for h in range(nh):
                    od_ref[h, t] = a[2 * h:2 * h + 1, :].astype(BF16)
                state = None
        if state is not None:
            for n, v in enumerate(state):
                dstate[n] = v
        prompt_work()
        refill = [(pos + n_slots, which) for pos, which in jobs]
        start_all(step, [(pos, which) for pos, which in refill if pos < per_step])
        wrapped = [(pos - per_step, which) for pos, which in refill if pos >= per_step]
        if wrapped:
            @pl.when(step + 1 < n_steps)
            def _():
                start_all(step + 1, wrapped)

    @pl.when(step == 0)
    def _():
        start_all(0, [(pos, which) for pos in range(n_slots) for which in (0, 1)])

    n2 = 2 * tq
    all_cols = tuple((c, c + ATTN_COL_BLOCK) for c in range(0, n2, ATTN_COL_BLOCK))
    late_cols = ((tq // 2, tq), (tq + tq // 2, n2))

    def stage_a(start, buf, diag_offset, cols=all_cols):
        kblk = k_ref[pl.ds(pl.multiple_of(start, tk), tk), :]
        for lo, hi in cols:
            s = jnp.dot(kblk, qz_scr[:, lo:hi], preferred_element_type=F32)
            if diag_offset is not None:
                key = lax.broadcasted_iota(jnp.int32, s.shape, 0) + diag_offset
                col = lax.broadcasted_iota(jnp.int32, s.shape, 1) + lo
                qry = jnp.where(col >= tq, col - tq, col)
                s = jnp.where(key <= qry, s, NEG)
            s_scr[buf, :, lo:hi] = s
            mb_scr[buf, :, lo:hi] = jnp.max(s, axis=0, keepdims=True)

    def stage_b(start, buf, cols=all_cols):
        vt = vt_ref[:, pl.ds(pl.multiple_of(start, tk), tk)]
        vt = jnp.concatenate([vt, jnp.ones((ONES_ROWS, tk), BF16)], axis=0)
        for lo, hi in cols:
            m_prev = m_scr[:, lo:hi]
            m_new = jnp.maximum(m_prev, mb_scr[buf, :, lo:hi])
            alpha = jnp.exp2(m_prev - m_new)
            p = jnp.exp2(s_scr[buf, :, lo:hi] - m_new)
            acc_scr[:, lo:hi] = alpha * acc_scr[:, lo:hi] + jnp.dot(vt, p.astype(BF16),
                                                                    preferred_element_type=F32)
            m_scr[:, lo:hi] = m_new

    diag0 = i * tq

    def prologue():
        qt = qt_ref[...]
        sub = lax.broadcasted_iota(jnp.int32, qt.shape, 0)
        zero = jnp.zeros_like(qt)
        qz_scr[:, 0:tq] = jnp.where(sub < LANES // 2, qt, zero)
        qz_scr[:, tq:n2] = jnp.where(sub >= LANES // 2, qt, zero)
        m_scr[...] = jnp.full(m_scr.shape, NEG, F32)
        acc_scr[...] = jnp.zeros(acc_scr.shape, F32)
        stage_a(diag0, 0, 0)

    def epilogue():
        before = jnp.where(i == 0, diag0, diag0 - tk)
        for n, c in enumerate(all_cols):
            if n < len(late_cols):
                stage_a(diag0 + tk, 1, tk, (late_cols[n],))
            stage_b(before, 0, (c,))
        stage_b(diag0 + tk, 1, late_cols)
        ot = (acc_scr[0:LANES, 0:tq] / acc_scr[LANES:LANES + 1, 0:tq]
              - lam * (acc_scr[0:LANES, tq:n2] / acc_scr[LANES:LANES + 1, tq:n2]))
        ms = jnp.mean(ot * ot, axis=0, keepdims=True)
        at = ot * lax.rsqrt(ms + EPS) * gcol_ref[...] * (1.0 - lam_init)
        o_ref[...] = at.T.astype(BF16)

    decode_point(0, prologue)
    decode_point(1)

    def pair(u, carry):
        first = 2 * u * tk
        before = jnp.where(u == 0, diag0, first - tk)
        for c in all_cols:
            stage_a(first, 1, None, (c,))
            stage_b(before, 0, (c,))
        for c in all_cols:
            stage_a(first + tk, 0, None, (c,))
            stage_b(first, 1, (c,))
        return carry

    lax.fori_loop(0, i, pair, 0)
    decode_point(2)
    decode_point(3, epilogue)


def _attention(pt_flat, lams, gsub, expand, qt, k, vt, qd, kn, vn, ck, cv,
               *, tq, tk, n_pages, cp, lam_init):
    b, nh, s, _ = k.shape
    n, _, aw = qd.shape
    hd = lams[0].shape[1]
    nq = s // tq
    n_steps = b * nh * nq
    assert n % n_steps == 0, "decode samples must split evenly over the attention grid steps"
    spp = n // n_steps
    per_step = spp * (n_pages // cp)
    n_slots = max(c for c in (2, 4, 8) if per_step % c == 0)
    body = functools.partial(_attn_body, tq=tq, tk=tk, n_pages=n_pages, cp=cp, spp=spp,
                             lam_init=lam_init)
    const = lambda shape: pl.BlockSpec(shape, lambda bi, h, i, pt: (0,) * len(shape))
    sample_rows = pl.BlockSpec((spp, 1, aw), lambda bi, h, i, pt: ((bi * nh + h) * nq + i, 0, 0))
    grid_spec = pltpu.PrefetchScalarGridSpec(
        num_scalar_prefetch=1,
        grid=(b, nh, nq),
        in_specs=[const((1, hd))] * 4 + [
            const((LANES, 1)), const((1, LANES)), const(expand.shape),
            pl.BlockSpec((None, None, LANES, tq), lambda bi, h, i, pt: (bi, h, 0, i)),
            pl.BlockSpec((None, None, s, LANES), lambda bi, h, i, pt: (bi, h, 0, 0)),
            pl.BlockSpec((None, None, LANES, s), lambda bi, h, i, pt: (bi, h, 0, 0)),
            sample_rows, sample_rows, sample_rows,
            pl.BlockSpec(memory_space=pl.ANY),
            pl.BlockSpec(memory_space=pl.ANY),
        ],
        out_specs=[
            pl.BlockSpec((None, None, tq, LANES), lambda bi, h, i, pt: (bi, h, i, 0)),
            pl.BlockSpec((nh, spp, 1, LANES), lambda bi, h, i, pt: (0, (bi * nh + h) * nq + i, 0, 0)),
        ],
        scratch_shapes=[
            pltpu.VMEM((LANES, 2 * tq), BF16),
            pltpu.VMEM((2, tk, 2 * tq), F32),
            pltpu.VMEM((2, 1, 2 * tq), F32),
            pltpu.VMEM((1, 2 * tq), F32),
            pltpu.VMEM((LANES + ONES_ROWS, 2 * tq), F32),
            pltpu.VMEM((n_slots, cp) + ck.shape[1:], F32),
            pltpu.VMEM((n_slots, cp) + cv.shape[1:], F32),
            pltpu.SemaphoreType.DMA((n_slots,)),
            pltpu.SemaphoreType.DMA((n_slots,)),
            pltpu.VMEM((3, SUBLANES, LANES), F32),
            pltpu.VMEM((2, cp * SUBLANES, LANES), F32),
            pltpu.VMEM((2, SUBLANES, LANES), F32),
        ],
    )
    return pl.pallas_call(
        body,
        grid_spec=grid_spec,
        out_shape=[jax.ShapeDtypeStruct((b, nh, s, LANES), BF16),
                   jax.ShapeDtypeStruct((nh, n, 1, LANES), BF16)],
        compiler_params=pltpu.CompilerParams(
            dimension_semantics=("arbitrary", "arbitrary", "arbitrary"),
            vmem_limit_bytes=VMEM_LIMIT_BYTES),
        name="attention",
    )(pt_flat, *lams, gsub.reshape(LANES, 1), gsub, expand, qt, k, vt, qd, kn, vn, ck, cv)


def _decode_query_rows(q):
    aw = q.shape[-1]
    row = lax.broadcasted_iota(jnp.int32, (SUBLANES, aw), 0)
    lane = lax.broadcasted_iota(jnp.int32, (SUBLANES, aw), 1)
    return jnp.where(lane // (LANES // 2) == row, q.astype(F32), 0.0)


def _decode_new_token(q, kn, vn):
    nh = q.shape[-1] // LANES
    m = jnp.sum(_decode_query_rows(q) * kn, axis=1, keepdims=True)
    m = jnp.broadcast_to(m, (SUBLANES, LANES))
    l = jnp.ones((SUBLANES, LANES), F32)
    row_v = lax.broadcasted_iota(jnp.int32, (SUBLANES, LANES), 0)
    acc = jnp.zeros((SUBLANES, LANES), F32)
    for h in range(nh):
        acc = jnp.where(row_v // 2 == h, vn[:, h * LANES:(h + 1) * LANES], acc)
    return m, l, acc


def _decode_scores(q, kpages, s_out, smax_out):
    cp = kpages.shape[0]
    qbd_b = _decode_query_rows(q).astype(BF16)
    s = [jnp.dot(qbd_b, kpages[p].astype(BF16), preferred_element_type=F32)
         for p in range(cp)]
    for p in range(cp):
        s_out[p * SUBLANES:(p + 1) * SUBLANES, :] = s[p]
    smax_out[...] = functools.reduce(jnp.maximum, s)


def _decode_values(s_in, smax_in, vpages, expand, state):
    m, l, acc = state
    cp = vpages.shape[0]
    nh = vpages.shape[1] // PAGE_SIZE
    m_new = jnp.maximum(m, jnp.max(smax_in[...], axis=1, keepdims=True))
    alpha = jnp.exp(m - m_new)
    p3 = jnp.exp(s_in[...].reshape(cp, SUBLANES, LANES) - m_new[None])
    l = alpha * l + jnp.sum(jnp.sum(p3, axis=0), axis=1, keepdims=True)
    p_all = p3.reshape(cp * SUBLANES, LANES)
    w_all = jnp.dot(p_all.astype(BF16), expand, preferred_element_type=F32)
    row_e = lax.broadcasted_iota(jnp.int32, (SUBLANES, nh * PAGE_SIZE), 0)
    lane_e = lax.broadcasted_iota(jnp.int32, (SUBLANES, nh * PAGE_SIZE), 1)
    own_head = lane_e % nh == row_e // 2
    pv = jnp.zeros((SUBLANES, LANES), F32)
    for p in range(cp):
        w_p = jnp.where(own_head, w_all[p * SUBLANES:(p + 1) * SUBLANES, :], 0.0).astype(BF16)
        pv = pv + jnp.dot(w_p, vpages[p].astype(BF16), preferred_element_type=F32)
    return m_new, l, alpha * acc + pv


def _decode_output(state, lam, gsub_row, lam_init):
    _, l, acc = state
    row_v = lax.broadcasted_iota(jnp.int32, (SUBLANES, LANES), 0)
    t = (acc / l) * jnp.where(row_v % 2 == 0, 1.0, -lam)
    o = t + pltpu.roll(t, SUBLANES - 1, 0)
    return _subln(o, gsub_row, 1.0 - lam_init)


def _finish_body(x_ref, a_ref, gbc_ref, wout_ref, gmlp_ref, wup_ref, wdown_ref, y_ref, *, fc):
    nh = a_ref.shape[0]
    mix = jnp.concatenate([a_ref[h] for h in range(nh)] + [gbc_ref[...]], axis=1)
    h = x_ref[...] + jnp.dot(mix, wout_ref[...], preferred_element_type=F32)
    ms = jnp.mean(h * h, axis=-1, keepdims=True)
    hn = (h * lax.rsqrt(ms + EPS) * gmlp_ref[...]).astype(BF16)
    y = h
    dff = wup_ref.shape[1]
    for c in range(dff // fc):
        z = jnp.dot(hn, wup_ref[:, c * fc:(c + 1) * fc], preferred_element_type=F32)
        r = jnp.maximum(z, 0.0)
        y = y + jnp.dot((r * r).astype(BF16), wdown_ref[c * fc:(c + 1) * fc, :],
                        preferred_element_type=F32)
    y_ref[...] = y


def _finish(x, a, gbc, wout, gmlp, wup, wdown, *, tm):
    b, s, d = x.shape
    nh = a.shape[1]
    cw = gbc.shape[2]
    dff = wup.shape[1]
    body = functools.partial(_finish_body, fc=_pick(dff, 1024))
    return pl.pallas_call(
        body,
        grid=(b, s // tm),
        in_specs=[
            pl.BlockSpec((None, tm, d), lambda i, j: (i, j, 0)),
            pl.BlockSpec((None, nh, tm, LANES), lambda i, j: (i, 0, j, 0)),
            pl.BlockSpec((None, tm, cw), lambda i, j: (i, j, 0)),
            _const_spec((d, d), 2),
            _const_spec((1, d), 2),
            _const_spec((d, dff), 2),
            _const_spec((dff, d), 2),
        ],
        out_specs=pl.BlockSpec((None, tm, d), lambda i, j: (i, j, 0)),
        out_shape=jax.ShapeDtypeStruct((b, s, d), F32),
        compiler_params=pltpu.CompilerParams(
            dimension_semantics=("arbitrary", "arbitrary"),
            vmem_limit_bytes=VMEM_LIMIT_BYTES),
        name="finish",
    )(x, a, gbc, wout, gmlp, wup, wdown)


def _rope_angles(pos, half):
    inv = ROPE_THETA ** (-jnp.arange(half, dtype=F32) / half)
    return pos.astype(F32)[:, None] * inv[None, :]


def _rope_tables(pos, half):
    ang = _rope_angles(pos, half)
    cos = jnp.cos(ang)
    sin = jnp.sin(ang)
    return (jnp.concatenate([cos, cos, cos, cos], axis=1),
            jnp.concatenate([-sin, sin, -sin, sin], axis=1))


def kernel(x_prompt, x_sample, cache_k, cache_v, state_conv, page_table, g_attn_norm, w_in, g_q,
           g_k, lam_q1, lam_k1, lam_q2, lam_k2, g_subln, w_conv, w_out, g_mlp_norm, w_up, w_down):
    b, s, d = x_prompt.shape
    db, dec_seq, _ = x_sample.shape
    assert dec_seq == 1, "decode path handles one new token per sample"
    depth = w_in.shape[0]
    hd = g_q.shape[-1]
    vd = g_subln.shape[-1]
    assert vd == LANES and 2 * hd == LANES, "one head must fill one 128-lane block"
    aw = d // 2
    cw = d - aw
    nh = aw // vd
    n_pool = cache_k.shape[1]
    n_pages = page_table.shape[1]
    past = n_pages * PAGE_SIZE
    assert cache_k.shape[2] == PAGE_SIZE

    tm = _pick(s, 1024)
    assert s % 2 == 0
    tk = _pick(s // 2, 512)
    tq = 2 * tk
    assert n_pages % 2 == 0
    cp = _pick(n_pages // 2, 16)

    ang_p = _rope_angles(jnp.arange(s), hd // 2).T
    cos_p, sin_p = jnp.cos(ang_p), jnp.sin(ang_p)
    cos_s, sin_s = _rope_tables(past + jnp.arange(1), hd // 2)
    grp = jnp.arange(aw) // hd
    gmat = ((grp[:, None] == grp[None, :]).astype(F32) / hd).astype(BF16)

    ck = jnp.transpose(cache_k, (0, 1, 3, 4, 5, 2)).reshape(depth * n_pool, aw, PAGE_SIZE)
    cv = cache_v.reshape(depth * n_pool, PAGE_SIZE * nh, vd)
    slot_of = jnp.arange(PAGE_SIZE * nh) // nh
    expand = (jnp.arange(PAGE_SIZE)[:, None] == slot_of[None, :]).astype(BF16)

    xp = x_prompt
    xs = x_sample.reshape(db, d)
    kp, vp, cpr, ksl, vsl, csl = [], [], [], [], [], []
    for l in range(depth):
        lam_init = 0.8 - 0.6 * math.exp(-0.3 * l)
        lams = tuple(v[l].reshape(1, hd).astype(F32) for v in (lam_q1, lam_k1, lam_q2, lam_k2))
        gat = g_attn_norm[l].reshape(1, d)
        gq = jnp.tile(g_q[l], aw // hd).reshape(1, aw)
        gk = jnp.tile(g_k[l], aw // hd).reshape(1, aw)
        gsub = g_subln[l].reshape(1, vd)
        gmlp = g_mlp_norm[l].reshape(1, d)
        win = w_in[l].astype(BF16)
        wout = w_out[l].astype(BF16)
        wup = w_up[l].astype(BF16)
        wdown = w_down[l].astype(BF16)
        wconv = w_conv[l]

        gq_rows = jnp.broadcast_to(jnp.tile(g_q[l], 2)[:, None], (LANES, LANES))
        gk_rows = jnp.broadcast_to(jnp.tile(g_k[l], 2)[:, None], (LANES, LANES))
        qt, kt, v, kb, vt, gbc, convp = _inproj_prompt(xp, cos_p, sin_p, gat, gq_rows, gk_rows,
                                                       win, wconv, tm=tm)
        qd, kn, vn, gbcd, convs = _inproj_decode(xs, cos_s, sin_s, gat, gq, gk, win, gmat, wconv,
                                                 state_conv[l].reshape(db, 2 * cw))
        pt_flat = (page_table + l * n_pool).reshape(-1).astype(jnp.int32)
        a, ad = _attention(pt_flat, lams, gsub, expand, qt, kb, vt, qd.reshape(db, 1, aw),
                           kn.reshape(db, 1, aw), vn.reshape(db, 1, aw), ck, cv,
                           tq=tq, tk=tk, n_pages=n_pages, cp=cp, lam_init=lam_init)
        xp = _finish(xp, a, gbc, wout, gmlp, wup, wdown, tm=tm)
        kp.append(jnp.transpose(kt.reshape(b, nh, 2, hd, s), (0, 4, 1, 2, 3)))
        vp.append(v.reshape(b, s, nh, vd))
        cpr.append(convp)
        xs = _finish(xs.reshape(1, db, d), ad.reshape(1, nh, db, LANES), gbcd.reshape(1, db, cw),
                     wout, gmlp, wup, wdown, tm=db).reshape(db, d)
        ksl.append(kn.reshape(db, 1, nh, 2, hd))
        vsl.append(vn.reshape(db, 1, nh, vd))
        csl.append(convs.reshape(db, 2, cw))

    return (xp, xs.reshape(db, 1, d), jnp.stack(kp), jnp.stack(vp), jnp.stack(cpr),
            jnp.stack(ksl), jnp.stack(vsl), jnp.stack(csl))
```
